```python
import math
import jax
import jax.numpy as jnp
from jax import lax
import numpy as np


D_MODEL = 1024
BATCH = 8
SEQ = 4096
DEPTH = 1

HEAD_DIM = 64
N_HEADS_SB = 8
N_HEADS_MB = 8
D_SB = N_HEADS_SB * HEAD_DIM
D_MB = N_HEADS_MB * HEAD_DIM
D_MIX = D_SB + D_MB
D_PLE = 256
SB_QBLOCK = 128
MOBA_BLOCK = 256
MOBA_TOPK = 3
MOBA_QCHUNK = 16
RMS_EPS = 1e-6

kernel_name = "hybrid_stickbreak_moba_layer"


def rmsnorm(x, g):
    xf = x.astype(jnp.float32)
    y = xf * lax.rsqrt(jnp.mean(xf * xf, axis=-1, keepdims=True) + RMS_EPS)
    return (y * g.astype(jnp.float32)).astype(x.dtype)


def split_heads(t, n_heads):
    b, s, _ = t.shape
    return t.reshape(b, s, n_heads, HEAD_DIM).transpose(0, 2, 1, 3)


def merge_heads(t):
    b, h, s, d = t.shape
    return t.transpose(0, 2, 1, 3).reshape(b, s, h * d)


def alibi_slopes(n_heads):
    return jnp.asarray(np.power(2.0, -8.0 * np.arange(1, n_heads + 1) / n_heads).astype(np.float32))


def stick_breaking_attention(q, k, v):
    b, h, s, d = q.shape
    scale = 1.0 / math.sqrt(d)
    qf = q.astype(jnp.float32)
    kf = k.astype(jnp.float32)
    vf = v.astype(jnp.float32)
    outs = []
    for i in range(s // SB_QBLOCK):
        t0 = i * SB_QBLOCK
        t1 = t0 + SB_QBLOCK
        z = jnp.einsum("bhqd,bhkd->bhqk", qf[:, :, t0:t1], kf[:, :, :t1]) * scale
        past = jnp.arange(t1)[None, :] < (t0 + jnp.arange(SB_QBLOCK))[:, None]
        log_keep = jnp.where(past, jax.nn.log_sigmoid(-z), 0.0)
        log_after = lax.cumsum(log_keep, axis=3, reverse=True) - log_keep
        w = jnp.where(past, jnp.exp(jax.nn.log_sigmoid(z) + log_after), 0.0)
        outs.append(jnp.einsum("bhqk,bhkd->bhqd", w, vf[:, :, :t1]))
    return jnp.concatenate(outs, axis=2).astype(v.dtype)


def moba_attention(q, k, v, slopes):
    b, h, s, d = q.shape
    scale = 1.0 / math.sqrt(d)
    n_blk = -(-s // MOBA_BLOCK)
    s_pad = n_blk * MOBA_BLOCK
    pad = ((0, 0), (0, 0), (0, s_pad - s), (0, 0))
    qf = jnp.pad(q.astype(jnp.float32), pad)
    kf = jnp.pad(k.astype(jnp.float32), pad)
    vf = jnp.pad(v.astype(jnp.float32), pad)
    kb = kf.reshape(b, h, n_blk, MOBA_BLOCK, d)
    vb = vf.reshape(b, h, n_blk, MOBA_BLOCK, d)

    k_mean = kb.mean(axis=3)
    gate = jnp.einsum("bhsd,bhnd->bhsn", qf, k_mean)
    q_blk = jnp.arange(s_pad) // MOBA_BLOCK
    fully_past = jnp.arange(n_blk)[None, :] < q_blk[:, None]
    gate = jnp.where(fully_past, gate, -jnp.inf)
    k_sel = min(MOBA_TOPK, n_blk)
    _, sel = lax.top_k(gate, k_sel)

    n_chunk = s_pad // MOBA_QCHUNK
    q_c = qf.reshape(b, h, n_chunk, MOBA_QCHUNK, d).transpose(2, 0, 1, 3, 4)
    sel_c = sel.reshape(b, h, n_chunk, MOBA_QCHUNK, k_sel).transpose(2, 0, 1, 3, 4)
    starts = jnp.arange(n_chunk, dtype=jnp.int32) * MOBA_QCHUNK
    b_ix = jnp.arange(b)[:, None, None, None]
    h_ix = jnp.arange(h)[None, :, None, None]
    offs = jnp.arange(MOBA_BLOCK)
    n_g = k_sel * MOBA_BLOCK

    def chunk(args):
        qc, sc, t0 = args
        q_pos = t0 + jnp.arange(MOBA_QCHUNK)
        blk = t0 // MOBA_BLOCK
        k_g = kb[b_ix, h_ix, sc]
        v_g = vb[b_ix, h_ix, sc]
        k_pos = sc[..., None] * MOBA_BLOCK + offs
        s_g = jnp.einsum("bhqd,bhqnkd->bhqnk", qc, k_g) * scale
        s_g = s_g - slopes[:, None, None, None] * (q_pos[:, None, None] - k_pos).astype(jnp.float32)
        s_g = jnp.where((jnp.arange(k_sel) < blk)[:, None], s_g, -jnp.inf)
        k_own = lax.dynamic_index_in_dim(kb, blk, axis=2, keepdims=False)
        v_own = lax.dynamic_index_in_dim(vb, blk, axis=2, keepdims=False)
        own_pos = blk * MOBA_BLOCK + offs
        s_o = jnp.einsum("bhqd,bhkd->bhqk", qc, k_own) * scale
        s_o = s_o - slopes[:, None, None] * (q_pos[:, None] - own_pos[None, :]).astype(jnp.float32)
        s_o = jnp.where(own_pos[None, :] <= q_pos[:, None], s_o, -jnp.inf)
        probs = jax.nn.softmax(
            jnp.concatenate([s_g.reshape(b, h, MOBA_QCHUNK, n_g), s_o], axis=-1), axis=-1)
        out = jnp.einsum("bhqn,bhqnd->bhqd", probs[..., :n_g],
                         v_g.reshape(b, h, MOBA_QCHUNK, n_g, d))
        out = out + jnp.einsum("bhqk,bhkd->bhqd", probs[..., n_g:], v_own)
        return out

    out = lax.map(chunk, (q_c, sel_c, starts))
    out = out.transpose(1, 2, 0, 3, 4).reshape(b, h, s_pad, d)[:, :, :s]
    return out.astype(v.dtype)


def setup_inputs(seed: int = 0) -> dict:
    key = jax.random.key(seed)
    ks = jax.random.split(key, 11)
    f32 = jnp.float32
    x = jax.random.normal(ks[0], (BATCH, SEQ, D_MODEL), f32)
    p = jax.random.normal(ks[1], (DEPTH, BATCH, SEQ, D_PLE), f32)
    w_in = jax.random.normal(ks[2], (DEPTH, D_MODEL, 4 * D_MIX), f32) * D_MODEL ** -0.5
    g_mix = 1.0 + 0.02 * jax.random.normal(ks[3], (DEPTH, D_MODEL), f32)
    g_out_sb = 1.0 + 0.02 * jax.random.normal(ks[4], (DEPTH, D_SB), f32)
    g_out_mb = 1.0 + 0.02 * jax.random.normal(ks[5], (DEPTH, D_MB), f32)
    w_out = jax.random.normal(ks[6], (DEPTH, D_MIX, D_MODEL), f32) * D_MIX ** -0.5
    w_ple = jax.random.normal(ks[7], (DEPTH, D_PLE, D_MODEL), f32) * D_PLE ** -0.5
    g_ple = 1.0 + 0.02 * jax.random.normal(ks[8], (DEPTH, D_MODEL), f32)
    w_ple_gate = jax.random.normal(ks[9], (DEPTH, D_MODEL, D_MODEL), f32) * D_MODEL ** -0.5
    g_final = 1.0 + 0.02 * jax.random.normal(ks[10], (D_MODEL,), f32)
    return {"x": x, "p": p, "w_in": w_in, "g_mix": g_mix, "g_out_sb": g_out_sb,
            "g_out_mb": g_out_mb, "w_out": w_out, "w_ple": w_ple, "g_ple": g_ple,
            "w_ple_gate": w_ple_gate, "g_final": g_final}


def reference(x, p, w_in, g_mix, g_out_sb, g_out_mb, w_out, w_ple, g_ple, w_ple_gate, g_final):
    slopes = alibi_slopes(N_HEADS_MB)
    widths = [D_SB] * 4 + [D_MB] * 4
    split_at = [int(o) for o in np.cumsum(widths)[:-1]]
    for i in range(DEPTH):
        h = rmsnorm(x, g_mix[i])
        proj = h @ w_in[i]
        q_sb, k_sb, v_sb, z_sb, q_mb, k_mb, v_mb, z_mb = jnp.split(proj, split_at, axis=-1)
        o_sb = merge_heads(stick_breaking_attention(
            split_heads(q_sb, N_HEADS_SB), split_heads(k_sb, N_HEADS_SB),
            split_heads(v_sb, N_HEADS_SB)))
        o_mb = merge_heads(moba_attention(
            split_heads(q_mb, N_HEADS_MB), split_heads(k_mb, N_HEADS_MB),
            split_heads(v_mb, N_HEADS_MB), slopes))
        y_sb = rmsnorm(o_sb, g_out_sb[i]) * jax.nn.silu(z_sb)
        y_mb = rmsnorm(o_mb, g_out_mb[i]) * jax.nn.silu(z_mb)
        x = x + jnp.concatenate([y_sb, y_mb], axis=-1) @ w_out[i]
        ple_gate = jax.nn.sigmoid(rmsnorm(x, g_ple[i]) @ w_ple_gate[i])
        x = x + (p[i] @ w_ple[i]) * ple_gate
    return rmsnorm(x, g_final)
```

```python
import functools
import math

import jax
import jax.numpy as jnp
import numpy as np
from jax import lax
from jax.experimental import pallas as pl
from jax.experimental.pallas import tpu as pltpu

HEAD_DIM = 64
N_HEADS = 8
D_GRP = N_HEADS * HEAD_DIM
PAIR = 2 * HEAD_DIM
N_PAIRS = D_GRP // PAIR
MOBA_BLOCK = 256
MOBA_TOPK = 3
RMS_EPS = 1e-6
SCALE = 1.0 / math.sqrt(HEAD_DIM)
NEG = -1e30
V7X_VMEM_LIMIT = 56 * 1024 * 1024

F32 = jnp.float32
BF16 = jnp.bfloat16

_NT = (((1,), (1,)), ((), ()))


def _rms(x, g):
    ms = jnp.mean(x * x, axis=-1, keepdims=True)
    return (x * lax.rsqrt(ms + RMS_EPS)) * g


def _proj_kernel(x_ref, g_ref, w_ref, o_ref):
    h = _rms(x_ref[...], g_ref[...])
    o_ref[...] = jnp.dot(h.astype(BF16), w_ref[...],
                         preferred_element_type=F32).astype(o_ref.dtype)


def _proj(x2d, g, w_bf16, tm):
    m, d = x2d.shape
    n = w_bf16.shape[1]
    return pl.pallas_call(
        _proj_kernel,
        grid=(m // tm,),
        in_specs=[
            pl.BlockSpec((tm, d), lambda i: (i, 0)),
            pl.BlockSpec((1, d), lambda i: (0, 0)),
            pl.BlockSpec((d, n), lambda i: (0, 0)),
        ],
        out_specs=pl.BlockSpec((tm, n), lambda i: (i, 0)),
        out_shape=jax.ShapeDtypeStruct((m, n), BF16),
        compiler_params=pltpu.CompilerParams(
            dimension_semantics=("arbitrary",), vmem_limit_bytes=V7X_VMEM_LIMIT),
    )(x2d, g, w_bf16)


def _sb_kernel(q_ref, k_ref, v_ref, o_ref, *, blk):
    qi = pl.program_id(2)
    lane = lax.broadcasted_iota(jnp.int32, (blk, PAIR), 1)
    q2 = q_ref[0] * SCALE
    zero = jnp.zeros_like(q2)
    qh = (jnp.where(lane < HEAD_DIM, q2, zero), jnp.where(lane >= HEAD_DIM, q2, zero))
    row = lax.broadcasted_iota(jnp.int32, (blk, blk), 0)
    col = lax.broadcasted_iota(jnp.int32, (blk, blk), 1)
    tri = jnp.where(row >= col, 1.0, 0.0).astype(BF16)
    past = col < row

    def block(j, carry, diag):
        start = pl.multiple_of(j * blk, blk)
        k2 = k_ref[0, pl.ds(start, blk), :]
        v2 = v_ref[0, pl.ds(start, blk), :]
        new = []
        for h in range(2):
            c, acc = carry[h]
            z = lax.dot_general(qh[h], k2, _NT, preferred_element_type=F32)
            lk = jnp.minimum(-z, 0.0) - jnp.log(1.0 + jnp.exp(-jnp.abs(z)))
            if diag:
                lk = jnp.where(past, lk, 0.0)
            hi = lk.astype(BF16)
            lo = (lk - hi.astype(F32)).astype(BF16)
            cs = (jnp.dot(hi, tri, preferred_element_type=F32)
                  + jnp.dot(lo, tri, preferred_element_type=F32)) + c
            w = jnp.exp(z + cs)
            if diag:
                w = jnp.where(past, w, 0.0)
            acc = acc + jnp.dot(w.astype(BF16), v2, preferred_element_type=F32)
            new.append((cs[:, 0:1], acc))
        return tuple(new)

    init = tuple((jnp.zeros((blk, 1), F32), jnp.zeros((blk, PAIR), F32)) for _ in range(2))
    carry = block(qi, init, True)
    carry = lax.fori_loop(0, qi, lambda i, c: block(qi - 1 - i, c, False), carry)
    o_ref[0] = jnp.where(lane < HEAD_DIM, carry[0][1], carry[1][1]).astype(o_ref.dtype)


def _sb_attention(proj, blk):
    b, s, _ = proj.shape
    kernel = functools.partial(_sb_kernel, blk=blk)
    return pl.pallas_call(
        kernel,
        grid=(b, N_PAIRS, s // blk),
        in_specs=[
            pl.BlockSpec((1, blk, PAIR), lambda bi, hp, qi: (bi, qi, hp)),
            pl.BlockSpec((1, s, PAIR), lambda bi, hp, qi: (bi, 0, N_PAIRS + hp)),
            pl.BlockSpec((1, s, PAIR), lambda bi, hp, qi: (bi, 0, 2 * N_PAIRS + hp)),
        ],
        out_specs=pl.BlockSpec((1, blk, PAIR), lambda bi, hp, qi: (bi, qi, hp)),
        out_shape=jax.ShapeDtypeStruct((b, s, D_GRP), F32),
        compiler_params=pltpu.CompilerParams(
            dimension_semantics=("arbitrary", "arbitrary", "arbitrary"),
            vmem_limit_bytes=V7X_VMEM_LIMIT),
    )(proj, proj, proj)


def _moba_kernel(slopes_ref, q_ref, k_ref, v_ref, o_ref, kmean_ref, selb_ref, *, n_blk):
    blk = MOBA_BLOCK
    hp = pl.program_id(1)
    qi = pl.program_id(2)
    lane = lax.broadcasted_iota(jnp.int32, (blk, PAIR), 1)

    @pl.when(qi == 0)
    def _():
        kmean_ref[...] = jnp.zeros_like(kmean_ref)
        for n in range(n_blk):
            kb = k_ref[0, n * blk:(n + 1) * blk, :].astype(F32)
            kmean_ref[n:n + 1, :] = jnp.mean(kb, axis=0, keepdims=True)

    q_raw = q_ref[0]
    q2 = q_raw * SCALE
    zero = jnp.zeros_like(q2)
    head_mask = (lane < HEAD_DIM, lane >= HEAD_DIM)
    qh = tuple(jnp.where(mk, q2, zero) for mk in head_mask)

    km = kmean_ref[...]
    lane_k = lax.broadcasted_iota(jnp.int32, km.shape, 1)
    n_idx = lax.broadcasted_iota(jnp.int32, (PAIR, blk), 0)
    for h in range(2):
        kmh = jnp.where((lane_k < HEAD_DIM) == (h == 0), km, 0.0)
        hi = kmh.astype(BF16)
        lo = (kmh - hi.astype(F32)).astype(BF16)
        gt = (lax.dot_general(hi, q_raw, _NT, preferred_element_type=F32)
              + lax.dot_general(lo, q_raw, _NT, preferred_element_type=F32))
        gt = jnp.where(n_idx < qi, gt, -jnp.inf)
        bias_t = jnp.zeros((PAIR, blk), F32)
        for j in range(n_blk):
            gj = gt[j:j + 1, :]
            beats = (jnp.where(gt > gj, 1.0, 0.0)
                     + jnp.where((gt == gj) & (n_idx < j), 1.0, 0.0))
            cnt = jnp.sum(beats, axis=0, keepdims=True)
            bias_t = jnp.where(n_idx == j, jnp.where(cnt < MOBA_TOPK, 0.0, NEG), bias_t)
        bias = bias_t.T
        for j in range(n_blk):
            selb_ref[h, j] = jnp.broadcast_to(bias[:, j:j + 1], (blk, PAIR))

    k_iota = lax.broadcasted_iota(jnp.int32, (1, blk), 1)
    row = lax.broadcasted_iota(jnp.int32, (blk, blk), 0)
    col = lax.broadcasted_iota(jnp.int32, (blk, blk), 1)
    causal = col <= row
    slopes = (slopes_ref[2 * hp], slopes_ref[2 * hp + 1])

    def scores(j, h):
        start = pl.multiple_of(j * blk, blk)
        k2 = k_ref[0, pl.ds(start, blk), :]
        s = lax.dot_general(qh[h], k2, _NT, preferred_element_type=F32)
        rel = (k_iota + (j - qi) * blk).astype(F32)
        return s + slopes[h] * rel

    def values(j):
        start = pl.multiple_of(j * blk, blk)
        return v_ref[0, pl.ds(start, blk), :]

    state = []
    v_own = values(qi)
    for h in range(2):
        s = jnp.where(causal, scores(qi, h), NEG)
        m = jnp.max(s, axis=1, keepdims=True)
        p = jnp.exp(s - m)
        l = jnp.sum(p, axis=1, keepdims=True)
        acc = jnp.dot(p.astype(BF16), v_own, preferred_element_type=F32)
        state.append((m, l, acc))

    def body(j, state):
        v2 = values(j)
        new = []
        for h in range(2):
            m, l, acc = state[h]
            sb = selb_ref[h, j]
            s = scores(j, h) + jnp.concatenate([sb] * (blk // PAIR), axis=1)
            m_new = jnp.maximum(m, jnp.max(s, axis=1, keepdims=True))
            alpha = jnp.exp(m - m_new)
            p = jnp.exp(s - m_new)
            l = alpha * l + jnp.sum(p, axis=1, keepdims=True)
            acc = alpha * acc + jnp.dot(p.astype(BF16), v2, preferred_element_type=F32)
            new.append((m_new, l, acc))
        return tuple(new)

    state = lax.fori_loop(0, qi, body, tuple(state))
    out = [state[h][2] / state[h][1] for h in range(2)]
    o_ref[0] = jnp.where(head_mask[0], out[0], out[1]).astype(o_ref.dtype)


def _moba_attention(proj, slopes):
    b, s, _ = proj.shape
    blk = MOBA_BLOCK
    n_blk = s // blk
    assert n_blk <= PAIR
    kernel = functools.partial(_moba_kernel, n_blk=n_blk)
    base = 4 * N_PAIRS
    return pl.pallas_call(
        kernel,
        grid=(b, N_PAIRS, n_blk),
        in_specs=[
            pl.BlockSpec(memory_space=pltpu.SMEM),
            pl.BlockSpec((1, blk, PAIR), lambda bi, hp, qi: (bi, qi, base + hp)),
            pl.BlockSpec((1, s, PAIR), lambda bi, hp, qi: (bi, 0, base + N_PAIRS + hp)),
            pl.BlockSpec((1, s, PAIR), lambda bi, hp, qi: (bi, 0, base + 2 * N_PAIRS + hp)),
        ],
        out_specs=pl.BlockSpec((1, blk, PAIR), lambda bi, hp, qi: (bi, qi, hp)),
        out_shape=jax.ShapeDtypeStruct((b, s, D_GRP), F32),
        scratch_shapes=[
            pltpu.VMEM((PAIR, PAIR), F32),
            pltpu.VMEM((2, n_blk, blk, PAIR), F32),
        ],
        compiler_params=pltpu.CompilerParams(
            dimension_semantics=("arbitrary", "arbitrary", "arbitrary"),
            vmem_limit_bytes=V7X_VMEM_LIMIT),
    )(slopes, proj, proj, proj)


def _out_kernel(x_ref, osb_ref, omb_ref, zsb_ref, zmb_ref, p_ref,
                gsb_ref, gmb_ref, wout_ref, gple_ref, wgate_ref, wple_ref, gfin_ref, o_ref):
    def gated(o_r, z_r, g_r):
        z = z_r[...].astype(F32)
        return _rms(o_r[...], g_r[...]) * (z * jax.nn.sigmoid(z))

    y = jnp.concatenate([gated(osb_ref, zsb_ref, gsb_ref),
                         gated(omb_ref, zmb_ref, gmb_ref)], axis=-1)
    x1 = x_ref[...] + jnp.dot(y.astype(BF16), wout_ref[...], preferred_element_type=F32)
    gate = jax.nn.sigmoid(jnp.dot(_rms(x1, gple_ref[...]).astype(BF16), wgate_ref[...],
                                  preferred_element_type=F32))
    ple = jnp.dot(p_ref[...].astype(BF16), wple_ref[...], preferred_element_type=F32)
    x2 = x1 + ple * gate
    o_ref[...] = _rms(x2, gfin_ref[...])


def _out_proj(x2d, o_sb, o_mb, proj2d, p2d, g_sb, g_mb, w_out, g_ple, w_gate, w_ple, g_fin, tm):
    m, d = x2d.shape
    d_ple = p2d.shape[1]
    row = lambda width, cb=0: pl.BlockSpec((tm, width), lambda i: (i, cb))
    full = lambda a: pl.BlockSpec(a.shape, lambda i: (0, 0))
    z_sb_block = 3
    z_mb_block = 7
    return pl.pallas_call(
        _out_kernel,
        grid=(m // tm,),
        in_specs=[
            row(d), row(D_GRP), row(D_GRP), row(D_GRP, z_sb_block), row(D_GRP, z_mb_block),
            row(d_ple), full(g_sb), full(g_mb), full(w_out), full(g_ple), full(w_gate),
            full(w_ple), full(g_fin),
        ],
        out_specs=row(d),
        out_shape=jax.ShapeDtypeStruct((m, d), F32),
        compiler_params=pltpu.CompilerParams(
            dimension_semantics=("arbitrary",), vmem_limit_bytes=V7X_VMEM_LIMIT),
    )(x2d, o_sb, o_mb, proj2d, proj2d, p2d, g_sb, g_mb, w_out, g_ple, w_gate, w_ple, g_fin)


def _alibi_slopes(n_heads):
    return jnp.asarray(np.power(2.0, -8.0 * np.arange(1, n_heads + 1) / n_heads).astype(np.float32))


def kernel(x, p, w_in, g_mix, g_out_sb, g_out_mb, w_out, w_ple, g_ple, w_ple_gate, g_final):
    b, s, d = x.shape
    assert w_in.shape[0] == 1, "single-layer trunk only"
    assert s % MOBA_BLOCK == 0
    m = b * s
    tm = 256
    slopes = _alibi_slopes(N_HEADS)
    x2d = x.reshape(m, d)
    proj = _proj(x2d, g_mix[0][None, :], w_in[0].astype(BF16), tm)
    proj3 = proj.reshape(b, s, proj.shape[1])
    o_sb = _sb_attention(proj3, MOBA_BLOCK).reshape(m, D_GRP)
    o_mb = _moba_attention(proj3, slopes).reshape(m, D_GRP)
    out = _out_proj(x2d, o_sb, o_mb, proj, p[0].reshape(m, -1),
                    g_out_sb[0][None, :], g_out_mb[0][None, :], w_out[0].astype(BF16),
                    g_ple[0][None, :], w_ple_gate[0].astype(BF16), w_ple[0].astype(BF16),
                    g_final[None, :], tm)
    return out.reshape(b, s, d)
```

```python
import functools
import math

import jax
import jax.numpy as jnp
import numpy as np
from jax import lax
from jax.experimental import pallas as pl
from jax.experimental.pallas import tpu as pltpu

HEAD_DIM = 64
N_HEADS = 8
D_GRP = N_HEADS * HEAD_DIM
PAIR = 2 * HEAD_DIM
N_PAIRS = D_GRP // PAIR
MOBA_BLOCK = 256
MOBA_TOPK = 3
MOBA_QSUB = 8
MOBA_MAX_BLOCKS = 16
SB_KBLOCK = 256
SB_QSUB = 8
RMS_EPS = 1e-6
SCALE = 1.0 / math.sqrt(HEAD_DIM)
LOG2E = 1.4426950408889634
NEG = -1e30
V7X_VMEM_LIMIT = 56 * 1024 * 1024

F32 = jnp.float32
BF16 = jnp.bfloat16

_NT = (((1,), (1,)), ((), ()))


def _rms(x, g):
    ms = jnp.mean(x * x, axis=-1, keepdims=True)
    return (x * lax.rsqrt(ms + RMS_EPS)) * g


def _split3(a):
    hi = a.astype(BF16)
    r1 = a - hi.astype(F32)
    mid = r1.astype(BF16)
    lo = (r1 - mid.astype(F32)).astype(BF16)
    return hi, mid, lo


def _proj_kernel(x_ref, g_ref, w_ref, o_ref):
    h = _rms(x_ref[...], g_ref[...])
    o_ref[...] = jnp.dot(h.astype(BF16), w_ref[...],
                         preferred_element_type=F32).astype(o_ref.dtype)


def _proj(x2d, g, w_bf16, tm):
    m, d = x2d.shape
    n = w_bf16.shape[1]
    return pl.pallas_call(
        _proj_kernel,
        grid=(m // tm,),
        in_specs=[
            pl.BlockSpec((tm, d), lambda i: (i, 0)),
            pl.BlockSpec((1, d), lambda i: (0, 0)),
            pl.BlockSpec((d, n), lambda i: (0, 0)),
        ],
        out_specs=pl.BlockSpec((tm, n), lambda i: (i, 0)),
        out_shape=jax.ShapeDtypeStruct((m, n), BF16),
        compiler_params=pltpu.CompilerParams(
            dimension_semantics=("arbitrary",), vmem_limit_bytes=V7X_VMEM_LIMIT),
    )(x2d, g, w_bf16)


def _sb_kernel(q_ref, k_ref, v_ref, o_ref, qh_ref, acc_ref, c_ref, *, n_sub):
    blk = SB_KBLOCK
    tq = n_sub * blk
    qi = pl.program_id(2)
    lane = lax.broadcasted_iota(jnp.int32, (tq, PAIR), 1)
    q2 = q_ref[0] * SCALE
    zero = jnp.zeros_like(q2)
    qh_ref[0] = jnp.where(lane < HEAD_DIM, q2, zero)
    qh_ref[1] = jnp.where(lane >= HEAD_DIM, q2, zero)
    acc_ref[...] = jnp.zeros_like(acc_ref)
    c_ref[...] = jnp.zeros_like(c_ref)
    row = lax.broadcasted_iota(jnp.int32, (2 * blk, blk), 0)
    col = lax.broadcasted_iota(jnp.int32, (2 * blk, blk), 1)
    tri2 = jnp.where((row & (blk - 1)) >= col, 1.0, 0.0).astype(BF16)

    def block(j, row0, diag):
        start = pl.multiple_of(j * blk, blk)
        k2 = k_ref[0, pl.ds(start, blk), :]
        v2 = v_ref[0, pl.ds(start, blk), :]
        rows = tq - row0
        if diag:
            past = (lax.broadcasted_iota(jnp.int32, (rows, blk), 1)
                    < lax.broadcasted_iota(jnp.int32, (rows, blk), 0))
        for h in range(2):
            z = lax.dot_general(qh_ref[h, row0:, :], k2, _NT, preferred_element_type=F32)
            sp = jnp.maximum(z, 0.0) + jnp.log(1.0 + jnp.exp2(jnp.abs(z) * (-LOG2E)))
            if diag:
                sp = jnp.where(past, sp, 0.0)
            hi = sp.astype(BF16)
            lo = (sp - hi.astype(F32)).astype(BF16)
            cs = jnp.dot(jnp.concatenate([hi, lo], axis=1), tri2,
                         preferred_element_type=F32) + c_ref[h, row0:, :]
            w = jnp.exp(z - cs)
            if diag:
                w = jnp.where(past, w, 0.0)
            acc_ref[h, row0:, :] += jnp.dot(w.astype(BF16), v2, preferred_element_type=F32)
            c_ref[h, row0:, :] = cs[:, 0:1]

    for d in reversed(range(n_sub)):
        block(qi * n_sub + d, d * blk, True)

    def body(i, carry):
        block(qi * n_sub - 1 - i, 0, False)
        return carry

    lax.fori_loop(0, qi * n_sub, body, 0)
    o_ref[0] = jnp.where(lane < HEAD_DIM, acc_ref[0], acc_ref[1]).astype(o_ref.dtype)


def _sb_attention(proj, n_sub):
    b, s, _ = proj.shape
    tq = n_sub * SB_KBLOCK
    assert s % tq == 0
    kernel = functools.partial(_sb_kernel, n_sub=n_sub)
    return pl.pallas_call(
        kernel,
        grid=(b, N_PAIRS, s // tq),
        in_specs=[
            pl.BlockSpec((1, tq, PAIR), lambda bi, hp, qi: (bi, qi, hp)),
            pl.BlockSpec((1, s, PAIR), lambda bi, hp, qi: (bi, 0, N_PAIRS + hp)),
            pl.BlockSpec((1, s, PAIR), lambda bi, hp, qi: (bi, 0, 2 * N_PAIRS + hp)),
        ],
        out_specs=pl.BlockSpec((1, tq, PAIR), lambda bi, hp, qi: (bi, qi, hp)),
        out_shape=jax.ShapeDtypeStruct((b, s, D_GRP), F32),
        scratch_shapes=[
            pltpu.VMEM((2, tq, PAIR), BF16),
            pltpu.VMEM((2, tq, PAIR), F32),
            pltpu.VMEM((2, tq, 1), F32),
        ],
        compiler_params=pltpu.CompilerParams(
            dimension_semantics=("arbitrary", "arbitrary", "arbitrary"),
            vmem_limit_bytes=V7X_VMEM_LIMIT),
    )(proj, proj, proj)


def _moba_kernel(slopes_ref, q_ref, k_ref, v_ref, o_ref,
                 kmean_ref, kaug_ref, vaug_ref, qaug_ref, m_ref, acc_ref, *, n_blk, n_sub):
    blk = MOBA_BLOCK
    tq = n_sub * blk
    hp = pl.program_id(1)
    qi = pl.program_id(2)
    ext0 = (HEAD_DIM, 0)

    @pl.when(qi == 0)
    def _():
        lane = lax.broadcasted_iota(jnp.int32, (blk, PAIR), 1)
        pos = lax.broadcasted_iota(jnp.int32, (blk, PAIR), 0)
        kmean_ref[...] = jnp.zeros_like(kmean_ref)
        for n in range(n_blk):
            k2 = k_ref[0, n * blk:(n + 1) * blk, :]
            v2 = v_ref[0, n * blk:(n + 1) * blk, :]
            kmean_ref[n:n + 1, :] = jnp.mean(k2.astype(F32), axis=0, keepdims=True)
            for h in range(2):
                own = (lane < HEAD_DIM) if h == 0 else (lane >= HEAD_DIM)
                rel = lane - ext0[h]
                a_hi, a_mid, a_lo = _split3(slopes_ref[2 * hp + h] * (pos + n * blk).astype(F32))
                ext = jnp.where(rel == n, 1.0, 0.0).astype(BF16)
                ext = jnp.where(rel == MOBA_MAX_BLOCKS, a_hi, ext)
                ext = jnp.where(rel == MOBA_MAX_BLOCKS + 1, a_mid, ext)
                ext = jnp.where(rel == MOBA_MAX_BLOCKS + 2, a_lo, ext)
                kaug_ref[h, n * blk:(n + 1) * blk, :] = jnp.where(own, k2, ext)
                vaug_ref[h, n * blk:(n + 1) * blk, :] = jnp.where(
                    own, v2, jnp.where(rel == 0, 1.0, 0.0).astype(BF16))

    q_raw = q_ref[0]
    q2 = q_raw * SCALE
    lane_q = lax.broadcasted_iota(jnp.int32, (tq, PAIR), 1)
    km = kmean_ref[...]
    lane_k = lax.broadcasted_iota(jnp.int32, km.shape, 1)
    n_idx = lax.broadcasted_iota(jnp.int32, (MOBA_MAX_BLOCKS, tq), 0)
    own_blk = qi * n_sub + lax.broadcasted_iota(jnp.int32, (MOBA_MAX_BLOCKS, tq), 1) // blk
    row128 = lax.broadcasted_iota(jnp.int32, (PAIR, tq), 0)
    for h in range(2):
        kmh = jnp.where((lane_k < HEAD_DIM) == (h == 0), km, 0.0)
        hi = kmh.astype(BF16)
        lo = (kmh - hi.astype(F32)).astype(BF16)
        gt = (lax.dot_general(hi, q_raw, _NT, preferred_element_type=F32)
              + lax.dot_general(lo, q_raw, _NT, preferred_element_type=F32))
        gt = jnp.where(n_idx < own_blk, gt, -jnp.inf)
        bias_t = jnp.zeros((MOBA_MAX_BLOCKS, tq), F32)
        for j in range(n_blk):
            gj = gt[j:j + 1, :]
            beats = (jnp.where(gt > gj, 1.0, 0.0)
                     + jnp.where((gt == gj) & (n_idx < j), 1.0, 0.0))
            cnt = jnp.sum(beats, axis=0, keepdims=True)
            own_row = own_blk[0:1, :]
            drop = ((cnt >= MOBA_TOPK) & (own_row > j)) | (own_row < j)
            bias_t = jnp.where((n_idx == j) & drop, NEG, bias_t)
        top = ext0[h]
        pieces = [bias_t, jnp.zeros((PAIR - top - MOBA_MAX_BLOCKS, tq), F32)]
        if top:
            pieces.insert(0, jnp.zeros((top, tq), F32))
        ext_t = jnp.concatenate(pieces, axis=0)
        rel = row128 - top
        ext_t = jnp.where((rel >= MOBA_MAX_BLOCKS) & (rel < MOBA_MAX_BLOCKS + 3), 1.0, ext_t)
        own = (lane_q < HEAD_DIM) if h == 0 else (lane_q >= HEAD_DIM)
        qaug_ref[h] = jnp.where(own, q2, ext_t.T.astype(BF16))

    m_ref[...] = jnp.full_like(m_ref, NEG)
    acc_ref[...] = jnp.zeros_like(acc_ref)

    def block(j, row0, diag):
        start = pl.multiple_of(j * blk, blk)
        rows = tq - row0
        if diag:
            causal = (lax.broadcasted_iota(jnp.int32, (rows, blk), 1)
                      <= lax.broadcasted_iota(jnp.int32, (rows, blk), 0))
        for h in range(2):
            ka = kaug_ref[h, pl.ds(start, blk), :]
            va = vaug_ref[h, pl.ds(start, blk), :]
            s = lax.dot_general(qaug_ref[h, row0:, :], ka, _NT, preferred_element_type=F32)
            if diag:
                s = jnp.where(causal, s, NEG)
            m_old = m_ref[h, row0:, :]
            m_new = jnp.maximum(m_old, jnp.max(s, axis=1, keepdims=True))
            alpha = jnp.exp(m_old - m_new)
            p = jnp.exp(s - jnp.concatenate([m_new] * (blk // PAIR), axis=1))
            acc_ref[h, row0:, :] = (alpha * acc_ref[h, row0:, :]
                                    + jnp.dot(p.astype(BF16), va, preferred_element_type=F32))
            m_ref[h, row0:, :] = m_new

    for d in reversed(range(n_sub)):
        block(qi * n_sub + d, d * blk, True)

    def body(i, carry):
        block(i, 0, False)
        return carry

    lax.fori_loop(0, qi * n_sub, body, 0)
    acc0 = acc_ref[0]
    acc1 = acc_ref[1]
    out0 = acc0 / acc0[:, HEAD_DIM:HEAD_DIM + 1]
    out1 = acc1 / acc1[:, 0:1]
    o_ref[0] = jnp.where(lane_q < HEAD_DIM, out0, out1).astype(o_ref.dtype)


def _moba_attention(proj, slopes, n_sub):
    b, s, _ = proj.shape
    blk = MOBA_BLOCK
    n_blk = s // blk
    tq = n_sub * blk
    assert n_blk <= MOBA_MAX_BLOCKS and s % tq == 0
    kernel = functools.partial(_moba_kernel, n_blk=n_blk, n_sub=n_sub)
    base = 4 * N_PAIRS
    return pl.pallas_call(
        kernel,
        grid=(b, N_PAIRS, s // tq),
        in_specs=[
            pl.BlockSpec(memory_space=pltpu.SMEM),
            pl.BlockSpec((1, tq, PAIR), lambda bi, hp, qi: (bi, qi, base + hp)),
            pl.BlockSpec((1, s, PAIR), lambda bi, hp, qi: (bi, 0, base + N_PAIRS + hp)),
            pl.BlockSpec((1, s, PAIR), lambda bi, hp, qi: (bi, 0, base + 2 * N_PAIRS + hp)),
        ],
        out_specs=pl.BlockSpec((1, tq, PAIR), lambda bi, hp, qi: (bi, qi, hp)),
        out_shape=jax.ShapeDtypeStruct((b, s, D_GRP), F32),
        scratch_shapes=[
            pltpu.VMEM((MOBA_MAX_BLOCKS, PAIR), F32),
            pltpu.VMEM((2, s, PAIR), BF16),
            pltpu.VMEM((2, s, PAIR), BF16),
            pltpu.VMEM((2, tq, PAIR), BF16),
            pltpu.VMEM((2, tq, PAIR), F32),
            pltpu.VMEM((2, tq, PAIR), F32),
        ],
        compiler_params=pltpu.CompilerParams(
            dimension_semantics=("arbitrary", "arbitrary", "arbitrary"),
            vmem_limit_bytes=V7X_VMEM_LIMIT),
    )(slopes, proj, proj, proj)


def _out_kernel(x_ref, osb_ref, omb_ref, zsb_ref, zmb_ref, p_ref,
                gsb_ref, gmb_ref, wout_ref, gple_ref, wgate_ref, wple_ref, gfin_ref, o_ref):
    def gated(o_r, z_r, g_r):
        z = z_r[...].astype(F32)
        return _rms(o_r[...], g_r[...]) * (z * jax.nn.sigmoid(z))

    y = jnp.concatenate([gated(osb_ref, zsb_ref, gsb_ref),
                         gated(omb_ref, zmb_ref, gmb_ref)], axis=-1)
    x1 = x_ref[...] + jnp.dot(y.astype(BF16), wout_ref[...], preferred_element_type=F32)
    gate = jax.nn.sigmoid(jnp.dot(_rms(x1, gple_ref[...]).astype(BF16), wgate_ref[...],
                                  preferred_element_type=F32))
    ple = jnp.dot(p_ref[...].astype(BF16), wple_ref[...], preferred_element_type=F32)
    x2 = x1 + ple * gate
    o_ref[...] = _rms(x2, gfin_ref[...])


def _out_proj(x2d, o_sb, o_mb, proj2d, p2d, g_sb, g_mb, w_out, g_ple, w_gate, w_ple, g_fin, tm):
    m, d = x2d.shape
    d_ple = p2d.shape[1]
    row = lambda width, cb=0: pl.BlockSpec((tm, width), lambda i: (i, cb))
    full = lambda a: pl.BlockSpec(a.shape, lambda i: (0, 0))
    z_sb_block = 3
    z_mb_block = 7
    return pl.pallas_call(
        _out_kernel,
        grid=(m // tm,),
        in_specs=[
            row(d), row(D_GRP), row(D_GRP), row(D_GRP, z_sb_block), row(D_GRP, z_mb_block),
            row(d_ple), full(g_sb), full(g_mb), full(w_out), full(g_ple), full(w_gate),
            full(w_ple), full(g_fin),
        ],
        out_specs=row(d),
        out_shape=jax.ShapeDtypeStruct((m, d), F32),
        compiler_params=pltpu.CompilerParams(
            dimension_semantics=("arbitrary",), vmem_limit_bytes=V7X_VMEM_LIMIT),
    )(x2d, o_sb, o_mb, proj2d, proj2d, p2d, g_sb, g_mb, w_out, g_ple, w_gate, w_ple, g_fin)


def _alibi_slopes(n_heads):
    return jnp.asarray(np.power(2.0, -8.0 * np.arange(1, n_heads + 1) / n_heads).astype(np.float32))


def kernel(x, p, w_in, g_mix, g_out_sb, g_out_mb, w_out, w_ple, g_ple, w_ple_gate, g_final):
    b, s, d = x.shape
    assert w_in.shape[0] == 1, "single-layer trunk only"
    m = b * s
    tm = 256
    slopes = _alibi_slopes(N_HEADS)
    x2d = x.reshape(m, d)
    proj = _proj(x2d, g_mix[0][None, :], w_in[0].astype(BF16), tm)
    proj3 = proj.reshape(b, s, proj.shape[1])
    o_sb = _sb_attention(proj3, min(SB_QSUB, s // SB_KBLOCK)).reshape(m, D_GRP)
    o_mb = _moba_attention(proj3, slopes, min(MOBA_QSUB, s // MOBA_BLOCK)).reshape(m, D_GRP)
    out = _out_proj(x2d, o_sb, o_mb, proj, p[0].reshape(m, -1),
                    g_out_sb[0][None, :], g_out_mb[0][None, :], w_out[0].astype(BF16),
                    g_ple[0][None, :], w_ple_gate[0].astype(BF16), w_ple[0].astype(BF16),
                    g_final[None, :], tm)
    return out.reshape(b, s, d)
```

```python
import functools
import math

import jax
import jax.numpy as jnp
import numpy as np
from jax import lax
from jax.experimental import pallas as pl
from jax.experimental.pallas import tpu as pltpu

HEAD_DIM = 64
N_HEADS = 8
D_GRP = N_HEADS * HEAD_DIM
PAIR = 2 * HEAD_DIM
N_PAIRS = D_GRP // PAIR
MOBA_BLOCK = 256
MOBA_TOPK = 3
MOBA_QSUB = 8
MOBA_MAX_BLOCKS = 16
SB_KBLOCK = 256
SB_QSUB = 8
SB_DEAD = 105.0
RMS_EPS = 1e-6
SCALE = 1.0 / math.sqrt(HEAD_DIM)
LOG2E = 1.4426950408889634
NEG = -1e30
V7X_VMEM_LIMIT = 56 * 1024 * 1024

F32 = jnp.float32
BF16 = jnp.bfloat16

_NT = (((1,), (1,)), ((), ()))


def _rms(x, g):
    ms = jnp.mean(x * x, axis=-1, keepdims=True)
    return (x * lax.rsqrt(ms + RMS_EPS)) * g


def _split3(a):
    hi = a.astype(BF16)
    r1 = a - hi.astype(F32)
    mid = r1.astype(BF16)
    lo = (r1 - mid.astype(F32)).astype(BF16)
    return hi, mid, lo


def _proj_kernel(x_ref, g_ref, w_ref, o_ref):
    h = _rms(x_ref[...], g_ref[...])
    o_ref[...] = jnp.dot(h.astype(BF16), w_ref[...],
                         preferred_element_type=F32).astype(o_ref.dtype)


def _proj(x2d, g, w_bf16, tm):
    m, d = x2d.shape
    n = w_bf16.shape[1]
    return pl.pallas_call(
        _proj_kernel,
        grid=(m // tm,),
        in_specs=[
            pl.BlockSpec((tm, d), lambda i: (i, 0)),
            pl.BlockSpec((1, d), lambda i: (0, 0)),
            pl.BlockSpec((d, n), lambda i: (0, 0)),
        ],
        out_specs=pl.BlockSpec((tm, n), lambda i: (i, 0)),
        out_shape=jax.ShapeDtypeStruct((m, n), BF16),
        compiler_params=pltpu.CompilerParams(
            dimension_semantics=("arbitrary",), vmem_limit_bytes=V7X_VMEM_LIMIT),
    )(x2d, g, w_bf16)


def _sb_kernel(q_ref, k_ref, v_ref, o_ref, qh_ref, acc_ref, c_ref, *, n_sub):
    blk = SB_KBLOCK
    tq = n_sub * blk
    qi = pl.program_id(2)
    lane = lax.broadcasted_iota(jnp.int32, (tq, PAIR), 1)
    q2 = q_ref[0] * SCALE
    zero = jnp.zeros_like(q2)
    qh_ref[0] = jnp.where(lane < HEAD_DIM, q2, zero)
    qh_ref[1] = jnp.where(lane >= HEAD_DIM, q2, zero)
    acc_ref[...] = jnp.zeros_like(acc_ref)
    c_ref[...] = jnp.zeros_like(c_ref)
    row = lax.broadcasted_iota(jnp.int32, (2 * blk, blk), 0)
    col = lax.broadcasted_iota(jnp.int32, (2 * blk, blk), 1)
    tri2 = jnp.where((row & (blk - 1)) >= col, 1.0, 0.0).astype(BF16)

    past = (lax.broadcasted_iota(jnp.int32, (blk, blk), 1)
            < lax.broadcasted_iota(jnp.int32, (blk, blk), 0))

    def mask_own(x, rows):
        parts = []
        for h in range(2):
            parts.append(jnp.where(past, x[h * rows:h * rows + blk], 0.0))
            if rows > blk:
                parts.append(x[h * rows + blk:(h + 1) * rows])
        return jnp.concatenate(parts, axis=0)

    def unit(j, row0, rows, diag):
        start = pl.multiple_of(j * blk, blk)
        k2 = k_ref[0, pl.ds(start, blk), :]
        v2 = v_ref[0, pl.ds(start, blk), :]
        rs = slice(row0, row0 + rows)
        both = lambda ref: jnp.concatenate([ref[0, rs, :], ref[1, rs, :]], axis=0)
        z = lax.dot_general(both(qh_ref), k2, _NT, preferred_element_type=F32)
        sp = jnp.maximum(z, 0.0) + jnp.log(1.0 + jnp.exp2(jnp.abs(z) * (-LOG2E)))
        if diag:
            sp = mask_own(sp, rows)
        hi = sp.astype(BF16)
        lo = (sp - hi.astype(F32)).astype(BF16)
        cs = jnp.dot(jnp.concatenate([hi, lo], axis=1), tri2,
                     preferred_element_type=F32) + both(c_ref)
        w = jnp.exp(z - cs)
        if diag:
            w = mask_own(w, rows)
        pv = jnp.dot(w.astype(BF16), v2, preferred_element_type=F32)
        for h in range(2):
            acc_ref[h, rs, :] += pv[h * rows:(h + 1) * rows]
            c_ref[h, rs, :] = cs[h * rows:(h + 1) * rows, 0:1]

    for r in reversed(range(n_sub)):
        unit(qi * n_sub + r, r * blk, min(2 * blk, tq - r * blk), True)

    @pl.when(qi > 0)
    def _():
        unit(qi * n_sub - 1, 0, blk, False)

    def alive(r):
        return jnp.min(c_ref[:, r * blk:(r + 1) * blk, :]) <= SB_DEAD

    @pl.when(jnp.min(c_ref[...]) <= SB_DEAD)
    def _():
        for r in range(n_sub):
            def cond(j, r=r):
                return jnp.logical_and(j >= 0, alive(r))

            def body(j, r=r):
                unit(j, r * blk, blk, False)
                return j - 1

            lax.while_loop(cond, body, qi * n_sub + r - 2)

    o_ref[0] = jnp.where(lane < HEAD_DIM, acc_ref[0], acc_ref[1]).astype(o_ref.dtype)


def _sb_attention(proj, n_sub):
    b, s, _ = proj.shape
    tq = n_sub * SB_KBLOCK
    assert s % tq == 0
    kernel = functools.partial(_sb_kernel, n_sub=n_sub)
    return pl.pallas_call(
        kernel,
        grid=(b, N_PAIRS, s // tq),
        in_specs=[
            pl.BlockSpec((1, tq, PAIR), lambda bi, hp, qi: (bi, qi, hp)),
            pl.BlockSpec((1, s, PAIR), lambda bi, hp, qi: (bi, 0, N_PAIRS + hp)),
            pl.BlockSpec((1, s, PAIR), lambda bi, hp, qi: (bi, 0, 2 * N_PAIRS + hp)),
        ],
        out_specs=pl.BlockSpec((1, tq, PAIR), lambda bi, hp, qi: (bi, qi, hp)),
        out_shape=jax.ShapeDtypeStruct((b, s, D_GRP), F32),
        scratch_shapes=[
            pltpu.VMEM((2, tq, PAIR), BF16),
            pltpu.VMEM((2, tq, PAIR), F32),
            pltpu.VMEM((2, tq, 1), F32),
        ],
        compiler_params=pltpu.CompilerParams(
            dimension_semantics=("arbitrary", "arbitrary", "arbitrary"),
            vmem_limit_bytes=V7X_VMEM_LIMIT),
    )(proj, proj, proj)


def _moba_kernel(slopes_ref, q_ref, k_ref, v_ref, o_ref,
                 kmean_ref, kaug_ref, vaug_ref, qaug_ref, m_ref, acc_ref, *, n_blk, n_sub):
    blk = MOBA_BLOCK
    tq = n_sub * blk
    hp = pl.program_id(1)
    qi = pl.program_id(2)
    ext0 = (HEAD_DIM, 0)

    @pl.when(qi == 0)
    def _():
        lane = lax.broadcasted_iota(jnp.int32, (blk, PAIR), 1)
        pos = lax.broadcasted_iota(jnp.int32, (blk, PAIR), 0)
        kmean_ref[...] = jnp.zeros_like(kmean_ref)
        for n in range(n_blk):
            k2 = k_ref[0, n * blk:(n + 1) * blk, :]
            v2 = v_ref[0, n * blk:(n + 1) * blk, :]
            kmean_ref[n:n + 1, :] = jnp.mean(k2.astype(F32), axis=0, keepdims=True)
            for h in range(2):
                own = (lane < HEAD_DIM) if h == 0 else (lane >= HEAD_DIM)
                rel = lane - ext0[h]
                a_hi, a_mid, a_lo = _split3(slopes_ref[2 * hp + h] * (pos + n * blk).astype(F32))
                ext = jnp.where(rel == n, 1.0, 0.0).astype(BF16)
                ext = jnp.where(rel == MOBA_MAX_BLOCKS, a_hi, ext)
                ext = jnp.where(rel == MOBA_MAX_BLOCKS + 1, a_mid, ext)
                ext = jnp.where(rel == MOBA_MAX_BLOCKS + 2, a_lo, ext)
                kaug_ref[h, n * blk:(n + 1) * blk, :] = jnp.where(own, k2, ext)
                vaug_ref[h, n * blk:(n + 1) * blk, :] = jnp.where(
                    own, v2, jnp.where(rel == 0, 1.0, 0.0).astype(BF16))

    q_raw = q_ref[0]
    q2 = q_raw * SCALE
    lane_q = lax.broadcasted_iota(jnp.int32, (tq, PAIR), 1)
    km = kmean_ref[...]
    lane_k = lax.broadcasted_iota(jnp.int32, km.shape, 1)
    n_idx = lax.broadcasted_iota(jnp.int32, (MOBA_MAX_BLOCKS, tq), 0)
    own_blk = qi * n_sub + lax.broadcasted_iota(jnp.int32, (MOBA_MAX_BLOCKS, tq), 1) // blk
    row128 = lax.broadcasted_iota(jnp.int32, (PAIR, tq), 0)
    for h in range(2):
        kmh = jnp.where((lane_k < HEAD_DIM) == (h == 0), km, 0.0)
        hi = kmh.astype(BF16)
        lo = (kmh - hi.astype(F32)).astype(BF16)
        gt = (lax.dot_general(hi, q_raw, _NT, preferred_element_type=F32)
              + lax.dot_general(lo, q_raw, _NT, preferred_element_type=F32))
        gt = jnp.where(n_idx < own_blk, gt, -jnp.inf)
        bias_t = jnp.zeros((MOBA_MAX_BLOCKS, tq), F32)
        for j in range(n_blk):
            gj = gt[j:j + 1, :]
            beats = (jnp.where(gt > gj, 1.0, 0.0)
                     + jnp.where((gt == gj) & (n_idx < j), 1.0, 0.0))
            cnt = jnp.sum(beats, axis=0, keepdims=True)
            own_row = own_blk[0:1, :]
            drop = ((cnt >= MOBA_TOPK) & (own_row > j)) | (own_row < j)
            bias_t = jnp.where((n_idx == j) & drop, NEG, bias_t)
        top = ext0[h]
        pieces = [bias_t, jnp.zeros((PAIR - top - MOBA_MAX_BLOCKS, tq), F32)]
        if top:
            pieces.insert(0, jnp.zeros((top, tq), F32))
        ext_t = jnp.concatenate(pieces, axis=0)
        rel = row128 - top
        ext_t = jnp.where((rel >= MOBA_MAX_BLOCKS) & (rel < MOBA_MAX_BLOCKS + 3), 1.0, ext_t)
        own = (lane_q < HEAD_DIM) if h == 0 else (lane_q >= HEAD_DIM)
        qaug_ref[h] = jnp.where(own, q2, ext_t.T.astype(BF16))

    m_ref[...] = jnp.full_like(m_ref, NEG)
    acc_ref[...] = jnp.zeros_like(acc_ref)
    causal = (lax.broadcasted_iota(jnp.int32, (blk, blk), 1)
              <= lax.broadcasted_iota(jnp.int32, (blk, blk), 0))

    def block(j, row0, diag):
        start = pl.multiple_of(j * blk, blk)
        rows = tq - row0
        for h in range(2):
            ka = kaug_ref[h, pl.ds(start, blk), :]
            va = vaug_ref[h, pl.ds(start, blk), :]
            s = lax.dot_general(qaug_ref[h, row0:, :], ka, _NT, preferred_element_type=F32)
            if diag:
                own = jnp.where(causal, s[:blk], NEG)
                s = own if rows == blk else jnp.concatenate([own, s[blk:]], axis=0)
            m_old = m_ref[h, row0:, :]
            m_new = jnp.maximum(m_old, jnp.max(s, axis=1, keepdims=True))
            alpha = jnp.exp(m_old - m_new)
            p = jnp.exp(s - jnp.concatenate([m_new] * (blk // PAIR), axis=1))
            acc_ref[h, row0:, :] = (alpha * acc_ref[h, row0:, :]
                                    + jnp.dot(p.astype(BF16), va, preferred_element_type=F32))
            m_ref[h, row0:, :] = m_new

    for d in reversed(range(n_sub)):
        block(qi * n_sub + d, d * blk, True)

    def body(i, carry):
        block(i, 0, False)
        return carry

    lax.fori_loop(0, qi * n_sub, body, 0)
    acc0 = acc_ref[0]
    acc1 = acc_ref[1]
    out0 = acc0 / acc0[:, HEAD_DIM:HEAD_DIM + 1]
    out1 = acc1 / acc1[:, 0:1]
    o_ref[0] = jnp.where(lane_q < HEAD_DIM, out0, out1).astype(o_ref.dtype)


def _moba_attention(proj, slopes, n_sub):
    b, s, _ = proj.shape
    blk = MOBA_BLOCK
    n_blk = s // blk
    tq = n_sub * blk
    assert n_blk <= MOBA_MAX_BLOCKS and s % tq == 0
    kernel = functools.partial(_moba_kernel, n_blk=n_blk, n_sub=n_sub)
    base = 4 * N_PAIRS
    return pl.pallas_call(
        kernel,
        grid=(b, N_PAIRS, s // tq),
        in_specs=[
            pl.BlockSpec(memory_space=pltpu.SMEM),
            pl.BlockSpec((1, tq, PAIR), lambda bi, hp, qi: (bi, qi, base + hp)),
            pl.BlockSpec((1, s, PAIR), lambda bi, hp, qi: (bi, 0, base + N_PAIRS + hp)),
            pl.BlockSpec((1, s, PAIR), lambda bi, hp, qi: (bi, 0, base + 2 * N_PAIRS + hp)),
        ],
        out_specs=pl.BlockSpec((1, tq, PAIR), lambda bi, hp, qi: (bi, qi, hp)),
        out_shape=jax.ShapeDtypeStruct((b, s, D_GRP), F32),
        scratch_shapes=[
            pltpu.VMEM((MOBA_MAX_BLOCKS, PAIR), F32),
            pltpu.VMEM((2, s, PAIR), BF16),
            pltpu.VMEM((2, s, PAIR), BF16),
            pltpu.VMEM((2, tq, PAIR), BF16),
            pltpu.VMEM((2, tq, PAIR), F32),
            pltpu.VMEM((2, tq, PAIR), F32),
        ],
        compiler_params=pltpu.CompilerParams(
            dimension_semantics=("arbitrary", "arbitrary", "arbitrary"),
            vmem_limit_bytes=V7X_VMEM_LIMIT),
    )(slopes, proj, proj, proj)


def _out_kernel(x_ref, osb_ref, omb_ref, zsb_ref, zmb_ref, p_ref,
                gsb_ref, gmb_ref, wout_ref, gple_ref, wgate_ref, wple_ref, gfin_ref, o_ref):
    def gated(o_r, z_r, g_r):
        z = z_r[...].astype(F32)
        return _rms(o_r[...], g_r[...]) * (z * jax.nn.sigmoid(z))

    y = jnp.concatenate([gated(osb_ref, zsb_ref, gsb_ref),
                         gated(omb_ref, zmb_ref, gmb_ref)], axis=-1)
    x1 = x_ref[...] + jnp.dot(y.astype(BF16), wout_ref[...], preferred_element_type=F32)
    gate = jax.nn.sigmoid(jnp.dot(_rms(x1, gple_ref[...]).astype(BF16), wgate_ref[...],
                                  preferred_element_type=F32))
    ple = jnp.dot(p_ref[...].astype(BF16), wple_ref[...], preferred_element_type=F32)
    x2 = x1 + ple * gate
    o_ref[...] = _rms(x2, gfin_ref[...])


def _out_proj(x2d, o_sb, o_mb, proj2d, p2d, g_sb, g_mb, w_out, g_ple, w_gate, w_ple, g_fin, tm):
    m, d = x2d.shape
    d_ple = p2d.shape[1]
    row = lambda width, cb=0: pl.BlockSpec((tm, width), lambda i: (i, cb))
    full = lambda a: pl.BlockSpec(a.shape, lambda i: (0, 0))
    z_sb_block = 3
    z_mb_block = 7
    return pl.pallas_call(
        _out_kernel,
        grid=(m // tm,),
        in_specs=[
            row(d), row(D_GRP), row(D_GRP), row(D_GRP, z_sb_block), row(D_GRP, z_mb_block),
            row(d_ple), full(g_sb), full(g_mb), full(w_out), full(g_ple), full(w_gate),
            full(w_ple), full(g_fin),
        ],
        out_specs=row(d),
        out_shape=jax.ShapeDtypeStruct((m, d), F32),
        compiler_params=pltpu.CompilerParams(
            dimension_semantics=("arbitrary",), vmem_limit_bytes=V7X_VMEM_LIMIT),
    )(x2d, o_sb, o_mb, proj2d, proj2d, p2d, g_sb, g_mb, w_out, g_ple, w_gate, w_ple, g_fin)


def _alibi_slopes(n_heads):
    return jnp.asarray(np.power(2.0, -8.0 * np.arange(1, n_heads + 1) / n_heads).astype(np.float32))


def kernel(x, p, w_in, g_mix, g_out_sb, g_out_mb, w_out, w_ple, g_ple, w_ple_gate, g_final):
    b, s, d = x.shape
    assert w_in.shape[0] == 1, "single-layer trunk only"
    m = b * s
    tm = 256
    slopes = _alibi_slopes(N_HEADS)
    x2d = x.reshape(m, d)
    proj = _proj(x2d, g_mix[0][None, :], w_in[0].astype(BF16), tm)
    proj3 = proj.reshape(b, s, proj.shape[1])
    o_sb = _sb_attention(proj3, min(SB_QSUB, s // SB_KBLOCK)).reshape(m, D_GRP)
    o_mb = _moba_attention(proj3, slopes, min(MOBA_QSUB, s // MOBA_BLOCK)).reshape(m, D_GRP)
    out = _out_proj(x2d, o_sb, o_mb, proj, p[0].reshape(m, -1),
                    g_out_sb[0][None, :], g_out_mb[0][None, :], w_out[0].astype(BF16),
                    g_ple[0][None, :], w_ple_gate[0].astype(BF16), w_ple[0].astype(BF16),
                    g_final[None, :], tm)
    return out.reshape(b, s, d)
```

```python
import functools
import math

import jax
import jax.numpy as jnp
import numpy as np
from jax import lax
from jax.experimental import pallas as pl
from jax.experimental.pallas import tpu as pltpu

HEAD_DIM = 64
N_HEADS = 8
D_GRP = N_HEADS * HEAD_DIM
PAIR = 2 * HEAD_DIM
N_PAIRS = D_GRP // PAIR
MOBA_BLOCK = 256
MOBA_TOPK = 3
MOBA_QSUB = 8
MOBA_UNROLL = 4
MOBA_MAX_BLOCKS = 16
SB_KBLOCK = 256
SB_QSUB = 16
SB_DEAD = 105.0
PROJ_ROWS = 256
OUT_ROWS = 512
RMS_EPS = 1e-6
SCALE = 1.0 / math.sqrt(HEAD_DIM)
LOG2E = 1.4426950408889634
NEG = -1e30
V7X_VMEM_LIMIT = 56 * 1024 * 1024

F32 = jnp.float32
BF16 = jnp.bfloat16

_NT = (((1,), (1,)), ((), ()))


def _rms(x, g):
    ms = jnp.mean(x * x, axis=-1, keepdims=True)
    return (x * lax.rsqrt(ms + RMS_EPS)) * g


def _split3(a):
    hi = a.astype(BF16)
    r1 = a - hi.astype(F32)
    mid = r1.astype(BF16)
    lo = (r1 - mid.astype(F32)).astype(BF16)
    return hi, mid, lo


def _proj_kernel(x_ref, g_ref, w_ref, o_ref):
    h = _rms(x_ref[...], g_ref[...])
    o_ref[...] = jnp.dot(h.astype(BF16), w_ref[...],
                         preferred_element_type=F32).astype(o_ref.dtype)


def _proj(x2d, g, w_bf16, tm):
    m, d = x2d.shape
    n = w_bf16.shape[1]
    return pl.pallas_call(
        _proj_kernel,
        grid=(m // tm,),
        in_specs=[
            pl.BlockSpec((tm, d), lambda i: (i, 0)),
            pl.BlockSpec((1, d), lambda i: (0, 0)),
            pl.BlockSpec((d, n), lambda i: (0, 0)),
        ],
        out_specs=pl.BlockSpec((tm, n), lambda i: (i, 0)),
        out_shape=jax.ShapeDtypeStruct((m, n), BF16),
        compiler_params=pltpu.CompilerParams(
            dimension_semantics=("arbitrary",), vmem_limit_bytes=V7X_VMEM_LIMIT),
    )(x2d, g, w_bf16)


def _sb_kernel(q_ref, k_ref, v_ref, o_ref, qh_ref, acc_ref, c_ref, *, n_sub):
    blk = SB_KBLOCK
    tq = n_sub * blk
    qi = pl.program_id(2)
    lane = lax.broadcasted_iota(jnp.int32, (tq, PAIR), 1)
    q2 = q_ref[0] * SCALE
    zero = jnp.zeros_like(q2)
    qh_ref[0] = jnp.where(lane < HEAD_DIM, q2, zero)
    qh_ref[1] = jnp.where(lane >= HEAD_DIM, q2, zero)
    acc_ref[...] = jnp.zeros_like(acc_ref)
    c_ref[...] = jnp.zeros_like(c_ref)
    row = lax.broadcasted_iota(jnp.int32, (2 * blk, blk), 0)
    col = lax.broadcasted_iota(jnp.int32, (2 * blk, blk), 1)
    tri2 = jnp.where((row & (blk - 1)) >= col, 1.0, 0.0).astype(BF16)

    past = (lax.broadcasted_iota(jnp.int32, (blk, blk), 1)
            < lax.broadcasted_iota(jnp.int32, (blk, blk), 0))

    def mask_own(x, rows):
        parts = []
        for h in range(2):
            parts.append(jnp.where(past, x[h * rows:h * rows + blk], 0.0))
            if rows > blk:
                parts.append(x[h * rows + blk:(h + 1) * rows])
        return jnp.concatenate(parts, axis=0)

    def unit(j, row0, rows, diag):
        start = pl.multiple_of(j * blk, blk)
        k2 = k_ref[0, pl.ds(start, blk), :]
        v2 = v_ref[0, pl.ds(start, blk), :]
        rs = pl.ds(row0, rows)
        both = lambda ref: jnp.concatenate([ref[0, rs, :], ref[1, rs, :]], axis=0)
        z = lax.dot_general(both(qh_ref), k2, _NT, preferred_element_type=F32)
        sp = jnp.maximum(z, 0.0) + jnp.log(1.0 + jnp.exp2(jnp.abs(z) * (-LOG2E)))
        if diag:
            sp = mask_own(sp, rows)
        hi = sp.astype(BF16)
        lo = (sp - hi.astype(F32)).astype(BF16)
        cs = jnp.dot(jnp.concatenate([hi, lo], axis=1), tri2,
                     preferred_element_type=F32) + both(c_ref)
        w = jnp.exp(z - cs)
        if diag:
            w = mask_own(w, rows)
        pv = jnp.dot(w.astype(BF16), v2, preferred_element_type=F32)
        for h in range(2):
            acc_ref[h, rs, :] += pv[h * rows:(h + 1) * rows]
            c_ref[h, rs, :] = cs[h * rows:(h + 1) * rows, 0:1]

    for r in reversed(range(n_sub)):
        unit(qi * n_sub + r, r * blk, min(2 * blk, tq - r * blk), True)

    @pl.when(qi > 0)
    def _():
        unit(qi * n_sub - 1, 0, blk, False)

    @pl.when(jnp.min(c_ref[...]) <= SB_DEAD)
    def _():
        def walk(r, carry):
            row0 = pl.multiple_of(r * blk, blk)

            def cond(j):
                alive = jnp.min(c_ref[:, pl.ds(row0, blk), :]) <= SB_DEAD
                return jnp.logical_and(j >= 0, alive)

            def body(j):
                unit(j, row0, blk, False)
                return j - 1

            lax.while_loop(cond, body, qi * n_sub + r - 2)
            return carry

        lax.fori_loop(0, n_sub, walk, 0)

    o_ref[0] = jnp.where(lane < HEAD_DIM, acc_ref[0], acc_ref[1]).astype(o_ref.dtype)


def _sb_attention(proj, n_sub):
    b, s, _ = proj.shape
    tq = n_sub * SB_KBLOCK
    assert s % tq == 0
    kernel = functools.partial(_sb_kernel, n_sub=n_sub)
    return pl.pallas_call(
        kernel,
        grid=(b, N_PAIRS, s // tq),
        in_specs=[
            pl.BlockSpec((1, tq, PAIR), lambda bi, hp, qi: (bi, qi, hp)),
            pl.BlockSpec((1, s, PAIR), lambda bi, hp, qi: (bi, 0, N_PAIRS + hp)),
            pl.BlockSpec((1, s, PAIR), lambda bi, hp, qi: (bi, 0, 2 * N_PAIRS + hp)),
        ],
        out_specs=pl.BlockSpec((1, tq, PAIR), lambda bi, hp, qi: (bi, qi, hp)),
        out_shape=jax.ShapeDtypeStruct((b, s, D_GRP), F32),
        scratch_shapes=[
            pltpu.VMEM((2, tq, PAIR), BF16),
            pltpu.VMEM((2, tq, PAIR), F32),
            pltpu.VMEM((2, tq, 1), F32),
        ],
        compiler_params=pltpu.CompilerParams(
            dimension_semantics=("arbitrary", "arbitrary", "arbitrary"),
            vmem_limit_bytes=V7X_VMEM_LIMIT),
    )(proj, proj, proj)


def _moba_kernel(slopes_ref, q_ref, k_ref, v_ref, o_ref,
                 kmean_ref, kaug_ref, vaug_ref, qaug_ref, m_ref, acc_ref, *, n_blk, n_sub):
    blk = MOBA_BLOCK
    tq = n_sub * blk
    hp = pl.program_id(1)
    qi = pl.program_id(2)
    ext0 = (HEAD_DIM, 0)

    @pl.when(qi == 0)
    def _():
        lane = lax.broadcasted_iota(jnp.int32, (blk, PAIR), 1)
        pos = lax.broadcasted_iota(jnp.int32, (blk, PAIR), 0)
        kmean_ref[...] = jnp.zeros_like(kmean_ref)
        for n in range(n_blk):
            k2 = k_ref[0, n * blk:(n + 1) * blk, :]
            v2 = v_ref[0, n * blk:(n + 1) * blk, :]
            kmean_ref[n:n + 1, :] = jnp.mean(k2.astype(F32), axis=0, keepdims=True)
            for h in range(2):
                own = (lane < HEAD_DIM) if h == 0 else (lane >= HEAD_DIM)
                rel = lane - ext0[h]
                a_hi, a_mid, a_lo = _split3(slopes_ref[2 * hp + h] * (pos + n * blk).astype(F32))
                ext = jnp.where(rel == n, 1.0, 0.0).astype(BF16)
                ext = jnp.where(rel == MOBA_MAX_BLOCKS, a_hi, ext)
                ext = jnp.where(rel == MOBA_MAX_BLOCKS + 1, a_mid, ext)
                ext = jnp.where(rel == MOBA_MAX_BLOCKS + 2, a_lo, ext)
                kaug_ref[h, n * blk:(n + 1) * blk, :] = jnp.where(own, k2, ext)
                vaug_ref[h, n * blk:(n + 1) * blk, :] = jnp.where(
                    own, v2, jnp.where(rel == 0, 1.0, 0.0).astype(BF16))

    q_raw = q_ref[0]
    q2 = q_raw * SCALE
    lane_q = lax.broadcasted_iota(jnp.int32, (tq, PAIR), 1)
    km = kmean_ref[...]
    lane_k = lax.broadcasted_iota(jnp.int32, km.shape, 1)
    n_idx = lax.broadcasted_iota(jnp.int32, (MOBA_MAX_BLOCKS, tq), 0)
    own_blk = qi * n_sub + lax.broadcasted_iota(jnp.int32, (MOBA_MAX_BLOCKS, tq), 1) // blk
    row128 = lax.broadcasted_iota(jnp.int32, (PAIR, tq), 0)
    for h in range(2):
        kmh = jnp.where((lane_k < HEAD_DIM) == (h == 0), km, 0.0)
        hi = kmh.astype(BF16)
        lo = (kmh - hi.astype(F32)).astype(BF16)
        gt = (lax.dot_general(hi, q_raw, _NT, preferred_element_type=F32)
              + lax.dot_general(lo, q_raw, _NT, preferred_element_type=F32))
        gt = jnp.where(n_idx < own_blk, gt, -jnp.inf)
        bias_t = jnp.zeros((MOBA_MAX_BLOCKS, tq), F32)
        for j in range(n_blk):
            gj = gt[j:j + 1, :]
            beats = (jnp.where(gt > gj, 1.0, 0.0)
                     + jnp.where((gt == gj) & (n_idx < j), 1.0, 0.0))
            cnt = jnp.sum(beats, axis=0, keepdims=True)
            own_row = own_blk[0:1, :]
            drop = ((cnt >= MOBA_TOPK) & (own_row > j)) | (own_row < j)
            bias_t = jnp.where((n_idx == j) & drop, NEG, bias_t)
        top = ext0[h]
        pieces = [bias_t, jnp.zeros((PAIR - top - MOBA_MAX_BLOCKS, tq), F32)]
        if top:
            pieces.insert(0, jnp.zeros((top, tq), F32))
        ext_t = jnp.concatenate(pieces, axis=0)
        rel = row128 - top
        ext_t = jnp.where((rel >= MOBA_MAX_BLOCKS) & (rel < MOBA_MAX_BLOCKS + 3), 1.0, ext_t)
        own = (lane_q < HEAD_DIM) if h == 0 else (lane_q >= HEAD_DIM)
        qaug_ref[h] = jnp.where(own, q2, ext_t.T.astype(BF16))

    m_ref[...] = jnp.full_like(m_ref, NEG)
    acc_ref[...] = jnp.zeros_like(acc_ref)
    causal = (lax.broadcasted_iota(jnp.int32, (blk, blk), 1)
              <= lax.broadcasted_iota(jnp.int32, (blk, blk), 0))

    def block(j, row0, diag):
        start = pl.multiple_of(j * blk, blk)
        rows = tq - row0
        for h in range(2):
            ka = kaug_ref[h, pl.ds(start, blk), :]
            va = vaug_ref[h, pl.ds(start, blk), :]
            s = lax.dot_general(qaug_ref[h, row0:, :], ka, _NT, preferred_element_type=F32)
            if diag:
                own = jnp.where(causal, s[:blk], NEG)
                s = own if rows == blk else jnp.concatenate([own, s[blk:]], axis=0)
            m_old = m_ref[h, row0:, :]
            m_new = jnp.maximum(m_old, jnp.max(s, axis=1, keepdims=True))
            alpha = jnp.exp(m_old - m_new)
            p = jnp.exp(s - jnp.concatenate([m_new] * (blk // PAIR), axis=1))
            acc_ref[h, row0:, :] = (alpha * acc_ref[h, row0:, :]
                                    + jnp.dot(p.astype(BF16), va, preferred_element_type=F32))
            m_ref[h, row0:, :] = m_new

    for d in reversed(range(n_sub)):
        block(qi * n_sub + d, d * blk, True)

    def body(i, carry):
        for u in range(MOBA_UNROLL):
            block(i * MOBA_UNROLL + u, 0, False)
        return carry

    if n_sub < n_blk:
        lax.fori_loop(0, qi * (n_sub // MOBA_UNROLL), body, 0)
    acc0 = acc_ref[0]
    acc1 = acc_ref[1]
    out0 = acc0 / acc0[:, HEAD_DIM:HEAD_DIM + 1]
    out1 = acc1 / acc1[:, 0:1]
    o_ref[0] = jnp.where(lane_q < HEAD_DIM, out0, out1).astype(o_ref.dtype)


def _moba_attention(proj, slopes, n_sub):
    b, s, _ = proj.shape
    blk = MOBA_BLOCK
    n_blk = s // blk
    tq = n_sub * blk
    assert n_blk <= MOBA_MAX_BLOCKS and s % tq == 0
    kernel = functools.partial(_moba_kernel, n_blk=n_blk, n_sub=n_sub)
    base = 4 * N_PAIRS
    return pl.pallas_call(
        kernel,
        grid=(b, N_PAIRS, s // tq),
        in_specs=[
            pl.BlockSpec(memory_space=pltpu.SMEM),
            pl.BlockSpec((1, tq, PAIR), lambda bi, hp, qi: (bi, qi, base + hp)),
            pl.BlockSpec((1, s, PAIR), lambda bi, hp, qi: (bi, 0, base + N_PAIRS + hp)),
            pl.BlockSpec((1, s, PAIR), lambda bi, hp, qi: (bi, 0, base + 2 * N_PAIRS + hp)),
        ],
        out_specs=pl.BlockSpec((1, tq, PAIR), lambda bi, hp, qi: (bi, qi, hp)),
        out_shape=jax.ShapeDtypeStruct((b, s, D_GRP), F32),
        scratch_shapes=[
            pltpu.VMEM((MOBA_MAX_BLOCKS, PAIR), F32),
            pltpu.VMEM((2, s, PAIR), BF16),
            pltpu.VMEM((2, s, PAIR), BF16),
            pltpu.VMEM((2, tq, PAIR), BF16),
            pltpu.VMEM((2, tq, PAIR), F32),
            pltpu.VMEM((2, tq, PAIR), F32),
        ],
        compiler_params=pltpu.CompilerParams(
            dimension_semantics=("arbitrary", "arbitrary", "arbitrary"),
            vmem_limit_bytes=V7X_VMEM_LIMIT),
    )(slopes, proj, proj, proj)


def _out_kernel(x_ref, osb_ref, omb_ref, zsb_ref, zmb_ref, p_ref,
                gsb_ref, gmb_ref, wout_ref, gple_ref, wgate_ref, wple_ref, gfin_ref, o_ref):
    def gated(o_r, z_r, g_r):
        z = z_r[...].astype(F32)
        return _rms(o_r[...], g_r[...]) * (z * jax.nn.sigmoid(z))

    y = jnp.concatenate([gated(osb_ref, zsb_ref, gsb_ref),
                         gated(omb_ref, zmb_ref, gmb_ref)], axis=-1)
    x1 = x_ref[...] + jnp.dot(y.astype(BF16), wout_ref[...], preferred_element_type=F32)
    gate = jax.nn.sigmoid(jnp.dot(_rms(x1, gple_ref[...]).astype(BF16), wgate_ref[...],
                                  preferred_element_type=F32))
    ple = jnp.dot(p_ref[...].astype(BF16), wple_ref[...], preferred_element_type=F32)
    x2 = x1 + ple * gate
    o_ref[...] = _rms(x2, gfin_ref[...])


def _out_proj(x2d, o_sb, o_mb, proj2d, p2d, g_sb, g_mb, w_out, g_ple, w_gate, w_ple, g_fin, tm):
    m, d = x2d.shape
    d_ple = p2d.shape[1]
    row = lambda width, cb=0: pl.BlockSpec((tm, width), lambda i: (i, cb))
    full = lambda a: pl.BlockSpec(a.shape, lambda i: (0, 0))
    z_sb_block = 3
    z_mb_block = 7
    return pl.pallas_call(
        _out_kernel,
        grid=(m // tm,),
        in_specs=[
            row(d), row(D_GRP), row(D_GRP), row(D_GRP, z_sb_block), row(D_GRP, z_mb_block),
            row(d_ple), full(g_sb), full(g_mb), full(w_out), full(g_ple), full(w_gate),
            full(w_ple), full(g_fin),
        ],
        out_specs=row(d),
        out_shape=jax.ShapeDtypeStruct((m, d), F32),
        compiler_params=pltpu.CompilerParams(
            dimension_semantics=("arbitrary",), vmem_limit_bytes=V7X_VMEM_LIMIT),
    )(x2d, o_sb, o_mb, proj2d, proj2d, p2d, g_sb, g_mb, w_out, g_ple, w_gate, w_ple, g_fin)


def _alibi_slopes(n_heads):
    return jnp.asarray(np.power(2.0, -8.0 * np.arange(1, n_heads + 1) / n_heads).astype(np.float32))


def kernel(x, p, w_in, g_mix, g_out_sb, g_out_mb, w_out, w_ple, g_ple, w_ple_gate, g_final):
    b, s, d = x.shape
    assert w_in.shape[0] == 1, "single-layer trunk only"
    m = b * s
    slopes = _alibi_slopes(N_HEADS)
    x2d = x.reshape(m, d)
    proj = _proj(x2d, g_mix[0][None, :], w_in[0].astype(BF16), PROJ_ROWS)
    proj3 = proj.reshape(b, s, proj.shape[1])
    o_sb = _sb_attention(proj3, min(SB_QSUB, s // SB_KBLOCK)).reshape(m, D_GRP)
    o_mb = _moba_attention(proj3, slopes, min(MOBA_QSUB, s // MOBA_BLOCK)).reshape(m, D_GRP)
    out = _out_proj(x2d, o_sb, o_mb, proj, p[0].reshape(m, -1),
                    g_out_sb[0][None, :], g_out_mb[0][None, :], w_out[0].astype(BF16),
                    g_ple[0][None, :], w_ple_gate[0].astype(BF16), w_ple[0].astype(BF16),
                    g_final[None, :], OUT_ROWS)
    return out.reshape(b, s, d)
```

```python
import functools
import math

import jax
import jax.numpy as jnp
import numpy as np
from jax import lax
from jax.experimental import pallas as pl
from jax.experimental.pallas import tpu as pltpu

HEAD_DIM = 64
N_HEADS = 8
D_GRP = N_HEADS * HEAD_DIM
PAIR = 2 * HEAD_DIM
N_PAIRS = D_GRP // PAIR
MOBA_BLOCK = 256
MOBA_TOPK = 3
MOBA_QSUB = 8
MOBA_UNROLL = 4
MOBA_MAX_BLOCKS = 16
SB_KBLOCK = 256
SB_QSUB = 16
SB_DEAD = 105.0
PROJ_ROWS = 256
OUT_ROWS = 512
RMS_EPS = 1e-6
SCALE = 1.0 / math.sqrt(HEAD_DIM)
LOG2E = 1.4426950408889634
NEG = -1e30
V7X_VMEM_LIMIT = 56 * 1024 * 1024

F32 = jnp.float32
BF16 = jnp.bfloat16

_NT = (((1,), (1,)), ((), ()))


def _rms(x, g):
    ms = jnp.mean(x * x, axis=-1, keepdims=True)
    return (x * lax.rsqrt(ms + RMS_EPS)) * g


def _split3(a):
    hi = a.astype(BF16)
    r1 = a - hi.astype(F32)
    mid = r1.astype(BF16)
    lo = (r1 - mid.astype(F32)).astype(BF16)
    return hi, mid, lo


def _proj_kernel(x_ref, g_ref, w_ref, o_ref):
    h = _rms(x_ref[...], g_ref[...])
    res = jnp.dot(h.astype(BF16), w_ref[...], preferred_element_type=F32).astype(o_ref.dtype)
    for cb in range(o_ref.shape[0]):
        o_ref[cb] = res[:, cb * PAIR:(cb + 1) * PAIR]


def _proj(x2d, g, w_bf16, b, tm):
    m, d = x2d.shape
    n = w_bf16.shape[1]
    s = m // b
    spb = s // tm
    return pl.pallas_call(
        _proj_kernel,
        grid=(m // tm,),
        in_specs=[
            pl.BlockSpec((tm, d), lambda i: (i, 0)),
            pl.BlockSpec((1, d), lambda i: (0, 0)),
            pl.BlockSpec((d, n), lambda i: (0, 0)),
        ],
        out_specs=pl.BlockSpec((None, n // PAIR, tm, PAIR), lambda i: (i // spb, 0, i % spb, 0)),
        out_shape=jax.ShapeDtypeStruct((b, n // PAIR, s, PAIR), BF16),
        compiler_params=pltpu.CompilerParams(
            dimension_semantics=("arbitrary",), vmem_limit_bytes=V7X_VMEM_LIMIT),
    )(x2d, g, w_bf16)


def _sb_kernel(q_ref, k_ref, v_ref, o_ref, qh_ref, acc_ref, c_ref, *, n_sub):
    blk = SB_KBLOCK
    tq = n_sub * blk
    qi = pl.program_id(2)
    lane = lax.broadcasted_iota(jnp.int32, (tq, PAIR), 1)
    q2 = q_ref[...] * SCALE
    zero = jnp.zeros_like(q2)
    qh_ref[0] = jnp.where(lane < HEAD_DIM, q2, zero)
    qh_ref[1] = jnp.where(lane >= HEAD_DIM, q2, zero)
    acc_ref[...] = jnp.zeros_like(acc_ref)
    c_ref[...] = jnp.zeros_like(c_ref)
    row = lax.broadcasted_iota(jnp.int32, (2 * blk, blk), 0)
    col = lax.broadcasted_iota(jnp.int32, (2 * blk, blk), 1)
    tri2 = jnp.where((row & (blk - 1)) >= col, 1.0, 0.0).astype(BF16)

    past = (lax.broadcasted_iota(jnp.int32, (blk, blk), 1)
            < lax.broadcasted_iota(jnp.int32, (blk, blk), 0))

    def mask_own(x, rows):
        parts = []
        for h in range(2):
            parts.append(jnp.where(past, x[h * rows:h * rows + blk], 0.0))
            if rows > blk:
                parts.append(x[h * rows + blk:(h + 1) * rows])
        return jnp.concatenate(parts, axis=0)

    def unit(j, row0, rows, diag):
        start = pl.multiple_of(j * blk, blk)
        k2 = k_ref[pl.ds(start, blk), :]
        v2 = v_ref[pl.ds(start, blk), :]
        rs = pl.ds(row0, rows)
        both = lambda ref: jnp.concatenate([ref[0, rs, :], ref[1, rs, :]], axis=0)
        z = lax.dot_general(both(qh_ref), k2, _NT, preferred_element_type=F32)
        sp = jnp.maximum(z, 0.0) + jnp.log(1.0 + jnp.exp2(jnp.abs(z) * (-LOG2E)))
        if diag:
            sp = mask_own(sp, rows)
        hi = sp.astype(BF16)
        lo = (sp - hi.astype(F32)).astype(BF16)
        cs = jnp.dot(jnp.concatenate([hi, lo], axis=1), tri2, preferred_element_type=F32)
        w = jnp.exp(z - cs)
        if diag:
            w = mask_own(w, rows)
        pv = jnp.dot(w.astype(BF16), v2, preferred_element_type=F32)
        for h in range(2):
            c_in = c_ref[h, rs, :]
            acc_ref[h, rs, :] += jnp.exp(-c_in) * pv[h * rows:(h + 1) * rows]
            c_ref[h, rs, :] = c_in + cs[h * rows:(h + 1) * rows, 0:1]

    for r in reversed(range(n_sub)):
        unit(qi * n_sub + r, r * blk, min(2 * blk, tq - r * blk), True)

    @pl.when(qi > 0)
    def _():
        unit(qi * n_sub - 1, 0, blk, False)

    @pl.when(jnp.min(c_ref[...]) <= SB_DEAD)
    def _():
        def walk(r, carry):
            row0 = pl.multiple_of(r * blk, blk)

            def cond(j):
                alive = jnp.min(c_ref[:, pl.ds(row0, blk), :]) <= SB_DEAD
                return jnp.logical_and(j >= 0, alive)

            def body(j):
                unit(j, row0, blk, False)
                return j - 1

            lax.while_loop(cond, body, qi * n_sub + r - 2)
            return carry

        lax.fori_loop(0, n_sub, walk, 0)

    o_ref[...] = jnp.where(lane < HEAD_DIM, acc_ref[0], acc_ref[1]).astype(o_ref.dtype)


def _slab(rows, group0):
    return pl.BlockSpec((None, None, rows, PAIR), lambda bi, hp, qi: (bi, group0 + hp, qi, 0))


def _sb_attention(proj, n_sub):
    b, _, s, _ = proj.shape
    tq = n_sub * SB_KBLOCK
    assert s % tq == 0
    kernel = functools.partial(_sb_kernel, n_sub=n_sub)
    whole = lambda group0: pl.BlockSpec((None, None, s, PAIR),
                                        lambda bi, hp, qi: (bi, group0 + hp, 0, 0))
    return pl.pallas_call(
        kernel,
        grid=(b, N_PAIRS, s // tq),
        in_specs=[_slab(tq, 0), whole(N_PAIRS), whole(2 * N_PAIRS)],
        out_specs=_slab(tq, 0),
        out_shape=jax.ShapeDtypeStruct((b, N_PAIRS, s, PAIR), F32),
        scratch_shapes=[
            pltpu.VMEM((2, tq, PAIR), BF16),
            pltpu.VMEM((2, tq, PAIR), F32),
            pltpu.VMEM((2, tq, 1), F32),
        ],
        compiler_params=pltpu.CompilerParams(
            dimension_semantics=("arbitrary", "arbitrary", "arbitrary"),
            vmem_limit_bytes=V7X_VMEM_LIMIT),
    )(proj, proj, proj)


def _moba_kernel(slopes_ref, q_ref, k_ref, v_ref, o_ref,
                 kmean_ref, kaug_ref, vaug_ref, qaug_ref, m_ref, acc_ref, *, n_blk, n_sub):
    blk = MOBA_BLOCK
    tq = n_sub * blk
    hp = pl.program_id(1)
    qi = pl.program_id(2)
    ext0 = (HEAD_DIM, 0)

    @pl.when(qi == 0)
    def _():
        lane = lax.broadcasted_iota(jnp.int32, (blk, PAIR), 1)
        pos = lax.broadcasted_iota(jnp.int32, (blk, PAIR), 0)
        kmean_ref[...] = jnp.zeros_like(kmean_ref)
        for n in range(n_blk):
            k2 = k_ref[n * blk:(n + 1) * blk, :]
            v2 = v_ref[n * blk:(n + 1) * blk, :]
            kmean_ref[n:n + 1, :] = jnp.mean(k2.astype(F32), axis=0, keepdims=True)
            for h in range(2):
                own = (lane < HEAD_DIM) if h == 0 else (lane >= HEAD_DIM)
                rel = lane - ext0[h]
                a_hi, a_mid, a_lo = _split3(slopes_ref[2 * hp + h] * (pos + n * blk).astype(F32))
                ext = jnp.where(rel == n, 1.0, 0.0).astype(BF16)
                ext = jnp.where(rel == MOBA_MAX_BLOCKS, a_hi, ext)
                ext = jnp.where(rel == MOBA_MAX_BLOCKS + 1, a_mid, ext)
                ext = jnp.where(rel == MOBA_MAX_BLOCKS + 2, a_lo, ext)
                kaug_ref[h, n * blk:(n + 1) * blk, :] = jnp.where(own, k2, ext)
                vaug_ref[h, n * blk:(n + 1) * blk, :] = jnp.where(
                    own, v2, jnp.where(rel == 0, 1.0, 0.0).astype(BF16))

    q_raw = q_ref[...]
    q2 = q_raw * SCALE
    lane_q = lax.broadcasted_iota(jnp.int32, (tq, PAIR), 1)
    km = kmean_ref[...]
    lane_k = lax.broadcasted_iota(jnp.int32, km.shape, 1)
    n_idx = lax.broadcasted_iota(jnp.int32, (MOBA_MAX_BLOCKS, tq), 0)
    own_blk = qi * n_sub + lax.broadcasted_iota(jnp.int32, (MOBA_MAX_BLOCKS, tq), 1) // blk
    row128 = lax.broadcasted_iota(jnp.int32, (PAIR, tq), 0)
    for h in range(2):
        kmh = jnp.where((lane_k < HEAD_DIM) == (h == 0), km, 0.0)
        hi = kmh.astype(BF16)
        lo = (kmh - hi.astype(F32)).astype(BF16)
        gt = (lax.dot_general(hi, q_raw, _NT, preferred_element_type=F32)
              + lax.dot_general(lo, q_raw, _NT, preferred_element_type=F32))
        gt = jnp.where(n_idx < own_blk, gt, -jnp.inf)
        bias_t = jnp.zeros((MOBA_MAX_BLOCKS, tq), F32)
        for j in range(n_blk):
            gj = gt[j:j + 1, :]
            beats = (jnp.where(gt > gj, 1.0, 0.0)
                     + jnp.where((gt == gj) & (n_idx < j), 1.0, 0.0))
            cnt = jnp.sum(beats, axis=0, keepdims=True)
            own_row = own_blk[0:1, :]
            drop = ((cnt >= MOBA_TOPK) & (own_row > j)) | (own_row < j)
            bias_t = jnp.where((n_idx == j) & drop, NEG, bias_t)
        top = ext0[h]
        pieces = [bias_t, jnp.zeros((PAIR - top - MOBA_MAX_BLOCKS, tq), F32)]
        if top:
            pieces.insert(0, jnp.zeros((top, tq), F32))
        ext_t = jnp.concatenate(pieces, axis=0)
        rel = row128 - top
        ext_t = jnp.where((rel >= MOBA_MAX_BLOCKS) & (rel < MOBA_MAX_BLOCKS + 3), 1.0, ext_t)
        own = (lane_q < HEAD_DIM) if h == 0 else (lane_q >= HEAD_DIM)
        qaug_ref[h] = jnp.where(own, q2, ext_t.T.astype(BF16))

    m_ref[...] = jnp.full_like(m_ref, NEG)
    acc_ref[...] = jnp.zeros_like(acc_ref)
    causal = (lax.broadcasted_iota(jnp.int32, (blk, blk), 1)
              <= lax.broadcasted_iota(jnp.int32, (blk, blk), 0))

    def block(j, row0, diag):
        start = pl.multiple_of(j * blk, blk)
        rows = tq - row0
        for h in range(2):
            ka = kaug_ref[h, pl.ds(start, blk), :]
            va = vaug_ref[h, pl.ds(start, blk), :]
            s = lax.dot_general(qaug_ref[h, row0:, :], ka, _NT, preferred_element_type=F32)
            if diag:
                own = jnp.where(causal, s[:blk], NEG)
                s = own if rows == blk else jnp.concatenate([own, s[blk:]], axis=0)
            m_old = m_ref[h, row0:, :]
            m_new = jnp.maximum(m_old, jnp.max(s, axis=1, keepdims=True))
            alpha = jnp.exp(m_old - m_new)
            p = jnp.exp(s - jnp.concatenate([m_new] * (blk // PAIR), axis=1))
            acc_ref[h, row0:, :] = (alpha * acc_ref[h, row0:, :]
                                    + jnp.dot(p.astype(BF16), va, preferred_element_type=F32))
            m_ref[h, row0:, :] = m_new

    for d in reversed(range(n_sub)):
        block(qi * n_sub + d, d * blk, True)

    def body(i, carry):
        for u in range(MOBA_UNROLL):
            block(i * MOBA_UNROLL + u, 0, False)
        return carry

    if n_sub < n_blk:
        lax.fori_loop(0, qi * (n_sub // MOBA_UNROLL), body, 0)
    acc0 = acc_ref[0]
    acc1 = acc_ref[1]
    out0 = acc0 / acc0[:, HEAD_DIM:HEAD_DIM + 1]
    out1 = acc1 / acc1[:, 0:1]
    o_ref[...] = jnp.where(lane_q < HEAD_DIM, out0, out1).astype(o_ref.dtype)


def _moba_attention(proj, slopes, n_sub):
    b, _, s, _ = proj.shape
    blk = MOBA_BLOCK
    n_blk = s // blk
    tq = n_sub * blk
    assert n_blk <= MOBA_MAX_BLOCKS and s % tq == 0
    kernel = functools.partial(_moba_kernel, n_blk=n_blk, n_sub=n_sub)
    base = 4 * N_PAIRS
    whole = lambda group0: pl.BlockSpec((None, None, s, PAIR),
                                        lambda bi, hp, qi: (bi, group0 + hp, 0, 0))
    return pl.pallas_call(
        kernel,
        grid=(b, N_PAIRS, s // tq),
        in_specs=[
            pl.BlockSpec(memory_space=pltpu.SMEM),
            _slab(tq, base), whole(base + N_PAIRS), whole(base + 2 * N_PAIRS),
        ],
        out_specs=_slab(tq, 0),
        out_shape=jax.ShapeDtypeStruct((b, N_PAIRS, s, PAIR), F32),
        scratch_shapes=[
            pltpu.VMEM((MOBA_MAX_BLOCKS, PAIR), F32),
            pltpu.VMEM((2, s, PAIR), BF16),
            pltpu.VMEM((2, s, PAIR), BF16),
            pltpu.VMEM((2, tq, PAIR), BF16),
            pltpu.VMEM((2, tq, PAIR), F32),
            pltpu.VMEM((2, tq, PAIR), F32),
        ],
        compiler_params=pltpu.CompilerParams(
            dimension_semantics=("arbitrary", "arbitrary", "arbitrary"),
            vmem_limit_bytes=V7X_VMEM_LIMIT),
    )(slopes, proj, proj, proj)


def _out_kernel(x_ref, osb_ref, omb_ref, zsb_ref, zmb_ref, p_ref,
                gsb_ref, gmb_ref, wout_ref, gple_ref, wgate_ref, wple_ref, gfin_ref, o_ref):
    def wide(r):
        return jnp.concatenate([r[c] for c in range(N_PAIRS)], axis=-1)

    def gated(o_r, z_r, g_r):
        z = wide(z_r).astype(F32)
        return _rms(wide(o_r), g_r[...]) * (z * jax.nn.sigmoid(z))

    y = jnp.concatenate([gated(osb_ref, zsb_ref, gsb_ref),
                         gated(omb_ref, zmb_ref, gmb_ref)], axis=-1)
    x1 = x_ref[...] + jnp.dot(y.astype(BF16), wout_ref[...], preferred_element_type=F32)
    gate = jax.nn.sigmoid(jnp.dot(_rms(x1, gple_ref[...]).astype(BF16), wgate_ref[...],
                                  preferred_element_type=F32))
    ple = jnp.dot(p_ref[...].astype(BF16), wple_ref[...], preferred_element_type=F32)
    x2 = x1 + ple * gate
    o_ref[...] = _rms(x2, gfin_ref[...])


def _out_proj(x2d, o_sb, o_mb, proj, p2d, g_sb, g_mb, w_out, g_ple, w_gate, w_ple, g_fin, tm):
    m, d = x2d.shape
    d_ple = p2d.shape[1]
    s = proj.shape[2]
    spb = s // tm
    row = lambda width: pl.BlockSpec((tm, width), lambda i: (i, 0))
    full = lambda a: pl.BlockSpec(a.shape, lambda i: (0, 0))
    groups = lambda gb: pl.BlockSpec((None, N_PAIRS, tm, PAIR),
                                     lambda i: (i // spb, gb, i % spb, 0))
    z_sb_block = 3
    z_mb_block = 7
    return pl.pallas_call(
        _out_kernel,
        grid=(m // tm,),
        in_specs=[
            row(d), groups(0), groups(0), groups(z_sb_block), groups(z_mb_block),
            row(d_ple), full(g_sb), full(g_mb), full(w_out), full(g_ple), full(w_gate),
            full(w_ple), full(g_fin),
        ],
        out_specs=row(d),
        out_shape=jax.ShapeDtypeStruct((m, d), F32),
        compiler_params=pltpu.CompilerParams(
            dimension_semantics=("arbitrary",), vmem_limit_bytes=V7X_VMEM_LIMIT),
    )(x2d, o_sb, o_mb, proj, proj, p2d, g_sb, g_mb, w_out, g_ple, w_gate, w_ple, g_fin)


def _alibi_slopes(n_heads):
    return jnp.asarray(np.power(2.0, -8.0 * np.arange(1, n_heads + 1) / n_heads).astype(np.float32))


def kernel(x, p, w_in, g_mix, g_out_sb, g_out_mb, w_out, w_ple, g_ple, w_ple_gate, g_final):
    b, s, d = x.shape
    assert w_in.shape[0] == 1, "single-layer trunk only"
    m = b * s
    slopes = _alibi_slopes(N_HEADS)
    x2d = x.reshape(m, d)
    proj = _proj(x2d, g_mix[0][None, :], w_in[0].astype(BF16), b, min(PROJ_ROWS, s))
    o_sb = _sb_attention(proj, min(SB_QSUB, s // SB_KBLOCK))
    o_mb = _moba_attention(proj, slopes, min(MOBA_QSUB, s // MOBA_BLOCK))
    out = _out_proj(x2d, o_sb, o_mb, proj, p[0].reshape(m, -1),
                    g_out_sb[0][None, :], g_out_mb[0][None, :], w_out[0].astype(BF16),
                    g_ple[0][None, :], w_ple_gate[0].astype(BF16), w_ple[0].astype(BF16),
                    g_final[None, :], min(OUT_ROWS, s))
    return out.reshape(b, s, d)
```

```python
import functools
import math

import jax
import jax.numpy as jnp
import numpy as np
from jax import lax
from jax.experimental import pallas as pl
from jax.experimental.pallas import tpu as pltpu

HEAD_DIM = 64
N_HEADS = 8
D_GRP = N_HEADS * HEAD_DIM
PAIR = 2 * HEAD_DIM
N_PAIRS = D_GRP // PAIR
MOBA_BLOCK = 256
MOBA_TOPK = 3
ATTN_QSUB = 8
MOBA_UNROLL = 4
MOBA_MAX_BLOCKS = 16
SB_KBLOCK = 256
SB_DEAD = 105.0
PROJ_ROWS = 256
OUT_ROWS = 512
RMS_EPS = 1e-6
SCALE = 1.0 / math.sqrt(HEAD_DIM)
LOG2E = 1.4426950408889634
NEG = -1e30
V7X_VMEM_LIMIT = 56 * 1024 * 1024

F32 = jnp.float32
BF16 = jnp.bfloat16

_NT = (((1,), (1,)), ((), ()))


def _rms(x, g):
    ms = jnp.mean(x * x, axis=-1, keepdims=True)
    return (x * lax.rsqrt(ms + RMS_EPS)) * g


def _split3(a):
    hi = a.astype(BF16)
    r1 = a - hi.astype(F32)
    mid = r1.astype(BF16)
    lo = (r1 - mid.astype(F32)).astype(BF16)
    return hi, mid, lo


def _proj_kernel(x_ref, g_ref, w_ref, o_ref):
    h = _rms(x_ref[...], g_ref[...])
    res = jnp.dot(h.astype(BF16), w_ref[...], preferred_element_type=F32).astype(o_ref.dtype)
    for cb in range(o_ref.shape[0]):
        o_ref[cb] = res[:, cb * PAIR:(cb + 1) * PAIR]


def _proj(x2d, g, w_bf16, b, tm):
    m, d = x2d.shape
    n = w_bf16.shape[1]
    s = m // b
    spb = s // tm
    return pl.pallas_call(
        _proj_kernel,
        grid=(m // tm,),
        in_specs=[
            pl.BlockSpec((tm, d), lambda i: (i, 0)),
            pl.BlockSpec((1, d), lambda i: (0, 0)),
            pl.BlockSpec((d, n), lambda i: (0, 0)),
        ],
        out_specs=pl.BlockSpec((None, n // PAIR, tm, PAIR), lambda i: (i // spb, 0, i % spb, 0)),
        out_shape=jax.ShapeDtypeStruct((b, n // PAIR, s, PAIR), BF16),
        compiler_params=pltpu.CompilerParams(
            dimension_semantics=("arbitrary",), vmem_limit_bytes=V7X_VMEM_LIMIT),
    )(x2d, g, w_bf16)


def _sb_phases(q_ref, k_ref, v_ref, o_ref, qh_ref, acc_ref, c_ref, *, n_sub):
    blk = SB_KBLOCK
    tq = n_sub * blk
    qi = pl.program_id(2)
    lane = lax.broadcasted_iota(jnp.int32, (tq, PAIR), 1)
    q2 = q_ref[...] * SCALE
    zero = jnp.zeros_like(q2)
    qh_ref[0] = jnp.where(lane < HEAD_DIM, q2, zero)
    qh_ref[1] = jnp.where(lane >= HEAD_DIM, q2, zero)
    acc_ref[...] = jnp.zeros_like(acc_ref)
    c_ref[...] = jnp.zeros_like(c_ref)
    row = lax.broadcasted_iota(jnp.int32, (2 * blk, blk), 0)
    col = lax.broadcasted_iota(jnp.int32, (2 * blk, blk), 1)
    tri2 = jnp.where((row & (blk - 1)) >= col, 1.0, 0.0).astype(BF16)

    past = (lax.broadcasted_iota(jnp.int32, (blk, blk), 1)
            < lax.broadcasted_iota(jnp.int32, (blk, blk), 0))

    def mask_own(x, rows):
        parts = []
        for h in range(2):
            parts.append(jnp.where(past, x[h * rows:h * rows + blk], 0.0))
            if rows > blk:
                parts.append(x[h * rows + blk:(h + 1) * rows])
        return jnp.concatenate(parts, axis=0)

    def unit(j, row0, rows, diag):
        start = pl.multiple_of(j * blk, blk)
        k2 = k_ref[pl.ds(start, blk), :]
        v2 = v_ref[pl.ds(start, blk), :]
        rs = pl.ds(row0, rows)
        both = lambda ref: jnp.concatenate([ref[0, rs, :], ref[1, rs, :]], axis=0)
        z = lax.dot_general(both(qh_ref), k2, _NT, preferred_element_type=F32)
        sp = jnp.maximum(z, 0.0) + jnp.log(1.0 + jnp.exp2(jnp.abs(z) * (-LOG2E)))
        if diag:
            sp = mask_own(sp, rows)
        hi = sp.astype(BF16)
        lo = (sp - hi.astype(F32)).astype(BF16)
        cs = jnp.dot(jnp.concatenate([hi, lo], axis=1), tri2, preferred_element_type=F32)
        w = jnp.exp(z - cs)
        if diag:
            w = mask_own(w, rows)
        pv = jnp.dot(w.astype(BF16), v2, preferred_element_type=F32)
        for h in range(2):
            c_in = c_ref[h, rs, :]
            acc_ref[h, rs, :] += jnp.exp(-c_in) * pv[h * rows:(h + 1) * rows]
            c_ref[h, rs, :] = c_in + cs[h * rows:(h + 1) * rows, 0:1]

    for r in reversed(range(n_sub)):
        unit(qi * n_sub + r, r * blk, min(2 * blk, tq - r * blk), True)

    yield

    @pl.when(qi > 0)
    def _():
        unit(qi * n_sub - 1, 0, blk, False)

    @pl.when(jnp.min(c_ref[...]) <= SB_DEAD)
    def _():
        def walk(r, carry):
            row0 = pl.multiple_of(r * blk, blk)

            def cond(j):
                alive = jnp.min(c_ref[:, pl.ds(row0, blk), :]) <= SB_DEAD
                return jnp.logical_and(j >= 0, alive)

            def body(j):
                unit(j, row0, blk, False)
                return j - 1

            lax.while_loop(cond, body, qi * n_sub + r - 2)
            return carry

        lax.fori_loop(0, n_sub, walk, 0)

    o_ref[...] = jnp.where(lane < HEAD_DIM, acc_ref[0], acc_ref[1]).astype(o_ref.dtype)


def _slab(rows, group0):
    return pl.BlockSpec((None, None, rows, PAIR), lambda bi, hp, qi: (bi, group0 + hp, qi, 0))


def _moba_phases(slopes_ref, q_ref, k_ref, v_ref, o_ref,
                 kmean_ref, kaug_ref, vaug_ref, qaug_ref, m_ref, acc_ref, *, n_blk, n_sub):
    blk = MOBA_BLOCK
    tq = n_sub * blk
    hp = pl.program_id(1)
    qi = pl.program_id(2)
    ext0 = (HEAD_DIM, 0)

    @pl.when(qi == 0)
    def _():
        lane = lax.broadcasted_iota(jnp.int32, (blk, PAIR), 1)
        pos = lax.broadcasted_iota(jnp.int32, (blk, PAIR), 0)
        kmean_ref[...] = jnp.zeros_like(kmean_ref)
        for n in range(n_blk):
            k2 = k_ref[n * blk:(n + 1) * blk, :]
            v2 = v_ref[n * blk:(n + 1) * blk, :]
            kmean_ref[n:n + 1, :] = jnp.mean(k2.astype(F32), axis=0, keepdims=True)
            for h in range(2):
                own = (lane < HEAD_DIM) if h == 0 else (lane >= HEAD_DIM)
                rel = lane - ext0[h]
                a_hi, a_mid, a_lo = _split3(slopes_ref[2 * hp + h] * (pos + n * blk).astype(F32))
                ext = jnp.where(rel == n, 1.0, 0.0).astype(BF16)
                ext = jnp.where(rel == MOBA_MAX_BLOCKS, a_hi, ext)
                ext = jnp.where(rel == MOBA_MAX_BLOCKS + 1, a_mid, ext)
                ext = jnp.where(rel == MOBA_MAX_BLOCKS + 2, a_lo, ext)
                kaug_ref[h, n * blk:(n + 1) * blk, :] = jnp.where(own, k2, ext)
                vaug_ref[h, n * blk:(n + 1) * blk, :] = jnp.where(
                    own, v2, jnp.where(rel == 0, 1.0, 0.0).astype(BF16))

    q_raw = q_ref[...]
    q2 = q_raw * SCALE
    lane_q = lax.broadcasted_iota(jnp.int32, (tq, PAIR), 1)
    km = kmean_ref[...]
    lane_k = lax.broadcasted_iota(jnp.int32, km.shape, 1)
    n_idx = lax.broadcasted_iota(jnp.int32, (MOBA_MAX_BLOCKS, tq), 0)
    own_blk = qi * n_sub + lax.broadcasted_iota(jnp.int32, (MOBA_MAX_BLOCKS, tq), 1) // blk
    row128 = lax.broadcasted_iota(jnp.int32, (PAIR, tq), 0)
    for h in range(2):
        kmh = jnp.where((lane_k < HEAD_DIM) == (h == 0), km, 0.0)
        hi = kmh.astype(BF16)
        lo = (kmh - hi.astype(F32)).astype(BF16)
        gt = (lax.dot_general(hi, q_raw, _NT, preferred_element_type=F32)
              + lax.dot_general(lo, q_raw, _NT, preferred_element_type=F32))
        gt = jnp.where(n_idx < own_blk, gt, -jnp.inf)
        bias_t = jnp.zeros((MOBA_MAX_BLOCKS, tq), F32)
        for j in range(n_blk):
            gj = gt[j:j + 1, :]
            beats = (jnp.where(gt > gj, 1.0, 0.0)
                     + jnp.where((gt == gj) & (n_idx < j), 1.0, 0.0))
            cnt = jnp.sum(beats, axis=0, keepdims=True)
            own_row = own_blk[0:1, :]
            drop = ((cnt >= MOBA_TOPK) & (own_row > j)) | (own_row < j)
            bias_t = jnp.where((n_idx == j) & drop, NEG, bias_t)
        top = ext0[h]
        pieces = [bias_t, jnp.zeros((PAIR - top - MOBA_MAX_BLOCKS, tq), F32)]
        if top:
            pieces.insert(0, jnp.zeros((top, tq), F32))
        ext_t = jnp.concatenate(pieces, axis=0)
        rel = row128 - top
        ext_t = jnp.where((rel >= MOBA_MAX_BLOCKS) & (rel < MOBA_MAX_BLOCKS + 3), 1.0, ext_t)
        own = (lane_q < HEAD_DIM) if h == 0 else (lane_q >= HEAD_DIM)
        qaug_ref[h] = jnp.where(own, q2, ext_t.T.astype(BF16))

    m_ref[...] = jnp.full_like(m_ref, NEG)
    acc_ref[...] = jnp.zeros_like(acc_ref)
    causal = (lax.broadcasted_iota(jnp.int32, (blk, blk), 1)
              <= lax.broadcasted_iota(jnp.int32, (blk, blk), 0))

    def block(j, row0, diag):
        start = pl.multiple_of(j * blk, blk)
        rows = tq - row0
        for h in range(2):
            ka = kaug_ref[h, pl.ds(start, blk), :]
            va = vaug_ref[h, pl.ds(start, blk), :]
            s = lax.dot_general(qaug_ref[h, row0:, :], ka, _NT, preferred_element_type=F32)
            if diag:
                own = jnp.where(causal, s[:blk], NEG)
                s = own if rows == blk else jnp.concatenate([own, s[blk:]], axis=0)
            m_old = m_ref[h, row0:, :]
            m_new = jnp.maximum(m_old, jnp.max(s, axis=1, keepdims=True))
            alpha = jnp.exp(m_old - m_new)
            p = jnp.exp(s - jnp.concatenate([m_new] * (blk // PAIR), axis=1))
            acc_ref[h, row0:, :] = (alpha * acc_ref[h, row0:, :]
                                    + jnp.dot(p.astype(BF16), va, preferred_element_type=F32))
            m_ref[h, row0:, :] = m_new

    for d in reversed(range(n_sub)):
        block(qi * n_sub + d, d * blk, True)

    yield

    def body(i, carry):
        for u in range(MOBA_UNROLL):
            block(i * MOBA_UNROLL + u, 0, False)
        return carry

    if n_sub < n_blk:
        lax.fori_loop(0, qi * (n_sub // MOBA_UNROLL), body, 0)
    acc0 = acc_ref[0]
    acc1 = acc_ref[1]
    out0 = acc0 / acc0[:, HEAD_DIM:HEAD_DIM + 1]
    out1 = acc1 / acc1[:, 0:1]
    o_ref[...] = jnp.where(lane_q < HEAD_DIM, out0, out1).astype(o_ref.dtype)


def _attn_kernel(slopes_ref, qs_ref, ks_ref, vs_ref, qm_ref, km_ref, vm_ref, os_ref, om_ref,
                 qh_ref, accs_ref, c_ref, kmean_ref, kaug_ref, vaug_ref, qaug_ref, m_ref,
                 accm_ref, *, n_blk, n_sub):
    moba = _moba_phases(slopes_ref, qm_ref, km_ref, vm_ref, om_ref, kmean_ref, kaug_ref,
                        vaug_ref, qaug_ref, m_ref, accm_ref, n_blk=n_blk, n_sub=n_sub)
    sb = _sb_phases(qs_ref, ks_ref, vs_ref, os_ref, qh_ref, accs_ref, c_ref, n_sub=n_sub)
    next(moba)
    next(sb)
    for _ in moba:
        pass
    for _ in sb:
        pass


def _attention(proj, slopes, n_sub):
    b, _, s, _ = proj.shape
    blk = MOBA_BLOCK
    n_blk = s // blk
    tq = n_sub * blk
    assert SB_KBLOCK == MOBA_BLOCK and n_blk <= MOBA_MAX_BLOCKS and s % tq == 0
    kernel = functools.partial(_attn_kernel, n_blk=n_blk, n_sub=n_sub)
    base = 4 * N_PAIRS
    whole = lambda group0: pl.BlockSpec((None, None, s, PAIR),
                                        lambda bi, hp, qi: (bi, group0 + hp, 0, 0))
    out = jax.ShapeDtypeStruct((b, N_PAIRS, s, PAIR), F32)
    return pl.pallas_call(
        kernel,
        grid=(b, N_PAIRS, s // tq),
        in_specs=[
            pl.BlockSpec(memory_space=pltpu.SMEM),
            _slab(tq, 0), whole(N_PAIRS), whole(2 * N_PAIRS),
            _slab(tq, base), whole(base + N_PAIRS), whole(base + 2 * N_PAIRS),
        ],
        out_specs=[_slab(tq, 0), _slab(tq, 0)],
        out_shape=[out, out],
        scratch_shapes=[
            pltpu.VMEM((2, tq, PAIR), BF16),
            pltpu.VMEM((2, tq, PAIR), F32),
            pltpu.VMEM((2, tq, 1), F32),
            pltpu.VMEM((MOBA_MAX_BLOCKS, PAIR), F32),
            pltpu.VMEM((2, s, PAIR), BF16),
            pltpu.VMEM((2, s, PAIR), BF16),
            pltpu.VMEM((2, tq, PAIR), BF16),
            pltpu.VMEM((2, tq, PAIR), F32),
            pltpu.VMEM((2, tq, PAIR), F32),
        ],
        compiler_params=pltpu.CompilerParams(
            dimension_semantics=("arbitrary", "arbitrary", "arbitrary"),
            vmem_limit_bytes=V7X_VMEM_LIMIT),
    )(slopes, proj, proj, proj, proj, proj, proj)


def _out_kernel(x_ref, osb_ref, omb_ref, zsb_ref, zmb_ref, p_ref,
                gsb_ref, gmb_ref, wout_ref, gple_ref, wgate_ref, wple_ref, gfin_ref, o_ref):
    def wide(r):
        return jnp.concatenate([r[c] for c in range(N_PAIRS)], axis=-1)

    def gated(o_r, z_r, g_r):
        z = wide(z_r).astype(F32)
        return _rms(wide(o_r), g_r[...]) * (z * jax.nn.sigmoid(z))

    y = jnp.concatenate([gated(osb_ref, zsb_ref, gsb_ref),
                         gated(omb_ref, zmb_ref, gmb_ref)], axis=-1)
    x1 = x_ref[...] + jnp.dot(y.astype(BF16), wout_ref[...], preferred_element_type=F32)
    gate = jax.nn.sigmoid(jnp.dot(_rms(x1, gple_ref[...]).astype(BF16), wgate_ref[...],
                                  preferred_element_type=F32))
    ple = jnp.dot(p_ref[...].astype(BF16), wple_ref[...], preferred_element_type=F32)
    x2 = x1 + ple * gate
    o_ref[...] = _rms(x2, gfin_ref[...])


def _out_proj(x2d, o_sb, o_mb, proj, p2d, g_sb, g_mb, w_out, g_ple, w_gate, w_ple, g_fin, tm):
    m, d = x2d.shape
    d_ple = p2d.shape[1]
    s = proj.shape[2]
    spb = s // tm
    row = lambda width: pl.BlockSpec((tm, width), lambda i: (i, 0))
    full = lambda a: pl.BlockSpec(a.shape, lambda i: (0, 0))
    groups = lambda gb: pl.BlockSpec((None, N_PAIRS, tm, PAIR),
                                     lambda i: (i // spb, gb, i % spb, 0))
    z_sb_block = 3
    z_mb_block = 7
    return pl.pallas_call(
        _out_kernel,
        grid=(m // tm,),
        in_specs=[
            row(d), groups(0), groups(0), groups(z_sb_block), groups(z_mb_block),
            row(d_ple), full(g_sb), full(g_mb), full(w_out), full(g_ple), full(w_gate),
            full(w_ple), full(g_fin),
        ],
        out_specs=row(d),
        out_shape=jax.ShapeDtypeStruct((m, d), F32),
        compiler_params=pltpu.CompilerParams(
            dimension_semantics=("arbitrary",), vmem_limit_bytes=V7X_VMEM_LIMIT),
    )(x2d, o_sb, o_mb, proj, proj, p2d, g_sb, g_mb, w_out, g_ple, w_gate, w_ple, g_fin)


def _alibi_slopes(n_heads):
    return jnp.asarray(np.power(2.0, -8.0 * np.arange(1, n_heads + 1) / n_heads).astype(np.float32))


def kernel(x, p, w_in, g_mix, g_out_sb, g_out_mb, w_out, w_ple, g_ple, w_ple_gate, g_final):
    b, s, d = x.shape
    assert w_in.shape[0] == 1, "single-layer trunk only"
    m = b * s
    slopes = _alibi_slopes(N_HEADS)
    x2d = x.reshape(m, d)
    proj = _proj(x2d, g_mix[0][None, :], w_in[0].astype(BF16), b, min(PROJ_ROWS, s))
    o_sb, o_mb = _attention(proj, slopes, min(ATTN_QSUB, s // MOBA_BLOCK))
    out = _out_proj(x2d, o_sb, o_mb, proj, p[0].reshape(m, -1),
                    g_out_sb[0][None, :], g_out_mb[0][None, :], w_out[0].astype(BF16),
                    g_ple[0][None, :], w_ple_gate[0].astype(BF16), w_ple[0].astype(BF16),
                    g_final[None, :], min(OUT_ROWS, s))
    return out.reshape(b, s, d)
```

```python
import functools
import math

import jax
import jax.numpy as jnp
import numpy as np
from jax import lax
from jax.experimental import pallas as pl
from jax.experimental.pallas import tpu as pltpu

HEAD_DIM = 64
N_HEADS = 8
D_GRP = N_HEADS * HEAD_DIM
PAIR = 2 * HEAD_DIM
N_PAIRS = D_GRP // PAIR
MOBA_BLOCK = 256
MOBA_TOPK = 3
ATTN_QSUB = 8
MOBA_UNROLL = 4
MOBA_MAX_BLOCKS = 16
SB_KBLOCK = 256
SB_DEAD = 105.0
PROJ_ROWS = 256
OUT_ROWS = 512
RMS_EPS = 1e-6
SCALE = 1.0 / math.sqrt(HEAD_DIM)
LOG2E = 1.4426950408889634
NEG = -1e30
V7X_VMEM_LIMIT = 56 * 1024 * 1024

F32 = jnp.float32
BF16 = jnp.bfloat16

_NT = (((1,), (1,)), ((), ()))


def _rms(x, g):
    ms = jnp.mean(x * x, axis=-1, keepdims=True)
    return (x * lax.rsqrt(ms + RMS_EPS)) * g


def _split3(a):
    hi = a.astype(BF16)
    r1 = a - hi.astype(F32)
    mid = r1.astype(BF16)
    lo = (r1 - mid.astype(F32)).astype(BF16)
    return hi, mid, lo


def _proj_kernel(x_ref, g_ref, w_ref, o_ref):
    h = _rms(x_ref[...], g_ref[...])
    res = jnp.dot(h.astype(BF16), w_ref[...], preferred_element_type=F32).astype(o_ref.dtype)
    for cb in range(o_ref.shape[0]):
        o_ref[cb] = res[:, cb * PAIR:(cb + 1) * PAIR]


def _proj(x2d, g, w_bf16, b, tm):
    m, d = x2d.shape
    n = w_bf16.shape[1]
    s = m // b
    spb = s // tm
    return pl.pallas_call(
        _proj_kernel,
        grid=(m // tm,),
        in_specs=[
            pl.BlockSpec((tm, d), lambda i: (i, 0)),
            pl.BlockSpec((1, d), lambda i: (0, 0)),
            pl.BlockSpec((d, n), lambda i: (0, 0)),
        ],
        out_specs=pl.BlockSpec((None, n // PAIR, tm, PAIR), lambda i: (i // spb, 0, i % spb, 0)),
        out_shape=jax.ShapeDtypeStruct((b, n // PAIR, s, PAIR), BF16),
        compiler_params=pltpu.CompilerParams(
            dimension_semantics=("arbitrary",), vmem_limit_bytes=V7X_VMEM_LIMIT),
    )(x2d, g, w_bf16)


def _sb_phases(q_ref, k_ref, v_ref, o_ref, qh_ref, acc_ref, c_ref, *, n_sub):
    blk = SB_KBLOCK
    tq = n_sub * blk
    qi = pl.program_id(2)
    lane = lax.broadcasted_iota(jnp.int32, (tq, PAIR), 1)
    q2 = q_ref[...] * SCALE
    zero = jnp.zeros_like(q2)
    qh_ref[0] = jnp.where(lane < HEAD_DIM, q2, zero)
    qh_ref[1] = jnp.where(lane >= HEAD_DIM, q2, zero)
    acc_ref[...] = jnp.zeros_like(acc_ref)
    c_ref[...] = jnp.zeros_like(c_ref)
    row = lax.broadcasted_iota(jnp.int32, (blk, blk), 0)
    col = lax.broadcasted_iota(jnp.int32, (blk, blk), 1)
    tri = jnp.where(row > col, 1.0, 0.0).astype(BF16)

    past = (lax.broadcasted_iota(jnp.int32, (blk, blk), 1)
            < lax.broadcasted_iota(jnp.int32, (blk, blk), 0))

    def mask_own(x, rows):
        parts = []
        for h in range(2):
            parts.append(jnp.where(past, x[h * rows:h * rows + blk], 0.0))
            if rows > blk:
                parts.append(x[h * rows + blk:(h + 1) * rows])
        return jnp.concatenate(parts, axis=0)

    def unit(j, row0, rows, diag):
        start = pl.multiple_of(j * blk, blk)
        k2 = k_ref[pl.ds(start, blk), :]
        v2 = v_ref[pl.ds(start, blk), :]
        rs = pl.ds(row0, rows)
        both = lambda ref: jnp.concatenate([ref[0, rs, :], ref[1, rs, :]], axis=0)
        z = lax.dot_general(both(qh_ref), k2, _NT, preferred_element_type=F32)
        sp = jnp.maximum(z, 0.0) + jnp.log(1.0 + jnp.exp2(jnp.abs(z) * (-LOG2E)))
        if diag:
            sp = mask_own(sp, rows)
        cs = jnp.dot(sp.astype(BF16), tri, preferred_element_type=F32)
        w = jnp.exp((z - sp) - cs)
        if diag:
            w = mask_own(w, rows)
        pv = jnp.dot(w.astype(BF16), v2, preferred_element_type=F32)
        total = cs[:, 0:1] + sp[:, 0:1]
        for h in range(2):
            c_in = c_ref[h, rs, :]
            acc_ref[h, rs, :] += jnp.exp(-c_in) * pv[h * rows:(h + 1) * rows]
            c_ref[h, rs, :] = c_in + total[h * rows:(h + 1) * rows]

    for r in reversed(range(n_sub)):
        unit(qi * n_sub + r, r * blk, min(2 * blk, tq - r * blk), True)

    yield

    @pl.when(qi > 0)
    def _():
        unit(qi * n_sub - 1, 0, blk, False)

    @pl.when(jnp.min(c_ref[...]) <= SB_DEAD)
    def _():
        def walk(r, carry):
            row0 = pl.multiple_of(r * blk, blk)

            def cond(j):
                alive = jnp.min(c_ref[:, pl.ds(row0, blk), :]) <= SB_DEAD
                return jnp.logical_and(j >= 0, alive)

            def body(j):
                unit(j, row0, blk, False)
                return j - 1

            lax.while_loop(cond, body, qi * n_sub + r - 2)
            return carry

        lax.fori_loop(0, n_sub, walk, 0)

    o_ref[...] = jnp.where(lane < HEAD_DIM, acc_ref[0], acc_ref[1]).astype(o_ref.dtype)


def _slab(rows, group0):
    return pl.BlockSpec((None, None, rows, PAIR), lambda bi, hp, qi: (bi, group0 + hp, qi, 0))


def _moba_phases(slopes_ref, q_ref, k_ref, v_ref, o_ref,
                 kmean_ref, kaug_ref, vaug_ref, qaug_ref, m_ref, acc_ref, *, n_blk, n_sub):
    blk = MOBA_BLOCK
    tq = n_sub * blk
    hp = pl.program_id(1)
    qi = pl.program_id(2)
    ext0 = (HEAD_DIM, 0)

    @pl.when(qi == 0)
    def _():
        lane = lax.broadcasted_iota(jnp.int32, (blk, PAIR), 1)
        pos = lax.broadcasted_iota(jnp.int32, (blk, PAIR), 0)
        kmean_ref[...] = jnp.zeros_like(kmean_ref)
        for n in range(n_blk):
            k2 = k_ref[n * blk:(n + 1) * blk, :]
            v2 = v_ref[n * blk:(n + 1) * blk, :]
            kmean_ref[n:n + 1, :] = jnp.mean(k2.astype(F32), axis=0, keepdims=True)
            for h in range(2):
                own = (lane < HEAD_DIM) if h == 0 else (lane >= HEAD_DIM)
                rel = lane - ext0[h]
                a_hi, a_mid, a_lo = _split3(slopes_ref[2 * hp + h] * (pos + n * blk).astype(F32))
                ext = jnp.where(rel == n, 1.0, 0.0).astype(BF16)
                ext = jnp.where(rel == MOBA_MAX_BLOCKS, a_hi, ext)
                ext = jnp.where(rel == MOBA_MAX_BLOCKS + 1, a_mid, ext)
                ext = jnp.where(rel == MOBA_MAX_BLOCKS + 2, a_lo, ext)
                kaug_ref[h, n * blk:(n + 1) * blk, :] = jnp.where(own, k2, ext)
                vaug_ref[h, n * blk:(n + 1) * blk, :] = jnp.where(
                    own, v2, jnp.where(rel == 0, 1.0, 0.0).astype(BF16))

    q_raw = q_ref[...]
    q2 = q_raw * SCALE
    lane_q = lax.broadcasted_iota(jnp.int32, (tq, PAIR), 1)
    km = kmean_ref[...]
    lane_k = lax.broadcasted_iota(jnp.int32, km.shape, 1)
    n_idx = lax.broadcasted_iota(jnp.int32, (MOBA_MAX_BLOCKS, tq), 0)
    own_blk = qi * n_sub + lax.broadcasted_iota(jnp.int32, (MOBA_MAX_BLOCKS, tq), 1) // blk
    row128 = lax.broadcasted_iota(jnp.int32, (PAIR, tq), 0)
    for h in range(2):
        kmh = jnp.where((lane_k < HEAD_DIM) == (h == 0), km, 0.0)
        hi = kmh.astype(BF16)
        lo = (kmh - hi.astype(F32)).astype(BF16)
        gt = (lax.dot_general(hi, q_raw, _NT, preferred_element_type=F32)
              + lax.dot_general(lo, q_raw, _NT, preferred_element_type=F32))
        gt = jnp.where(n_idx < own_blk, gt, -jnp.inf)
        bias_t = jnp.zeros((MOBA_MAX_BLOCKS, tq), F32)
        for j in range(n_blk):
            gj = gt[j:j + 1, :]
            beats = (jnp.where(gt > gj, 1.0, 0.0)
                     + jnp.where((gt == gj) & (n_idx < j), 1.0, 0.0))
            cnt = jnp.sum(beats, axis=0, keepdims=True)
            own_row = own_blk[0:1, :]
            drop = ((cnt >= MOBA_TOPK) & (own_row > j)) | (own_row < j)
            bias_t = jnp.where((n_idx == j) & drop, NEG, bias_t)
        top = ext0[h]
        pieces = [bias_t, jnp.zeros((PAIR - top - MOBA_MAX_BLOCKS, tq), F32)]
        if top:
            pieces.insert(0, jnp.zeros((top, tq), F32))
        ext_t = jnp.concatenate(pieces, axis=0)
        rel = row128 - top
        ext_t = jnp.where((rel >= MOBA_MAX_BLOCKS) & (rel < MOBA_MAX_BLOCKS + 3), 1.0, ext_t)
        own = (lane_q < HEAD_DIM) if h == 0 else (lane_q >= HEAD_DIM)
        qaug_ref[h] = jnp.where(own, q2, ext_t.T.astype(BF16))

    m_ref[...] = jnp.full_like(m_ref, NEG)
    acc_ref[...] = jnp.zeros_like(acc_ref)
    causal = (lax.broadcasted_iota(jnp.int32, (blk, blk), 1)
              <= lax.broadcasted_iota(jnp.int32, (blk, blk), 0))

    def block(j, row0, diag):
        start = pl.multiple_of(j * blk, blk)
        rows = tq - row0
        for h in range(2):
            ka = kaug_ref[h, pl.ds(start, blk), :]
            va = vaug_ref[h, pl.ds(start, blk), :]
            s = lax.dot_general(qaug_ref[h, row0:, :], ka, _NT, preferred_element_type=F32)
            if diag:
                own = jnp.where(causal, s[:blk], NEG)
                s = own if rows == blk else jnp.concatenate([own, s[blk:]], axis=0)
            m_old = m_ref[h, row0:, :]
            m_new = jnp.maximum(m_old, jnp.max(s, axis=1, keepdims=True))
            alpha = jnp.exp(m_old - m_new)
            p = jnp.exp(s - jnp.concatenate([m_new] * (blk // PAIR), axis=1))
            acc_ref[h, row0:, :] = (alpha * acc_ref[h, row0:, :]
                                    + jnp.dot(p.astype(BF16), va, preferred_element_type=F32))
            m_ref[h, row0:, :] = m_new

    for d in reversed(range(n_sub)):
        block(qi * n_sub + d, d * blk, True)

    yield

    def body(i, carry):
        for u in range(MOBA_UNROLL):
            block(i * MOBA_UNROLL + u, 0, False)
        return carry

    if n_sub < n_blk:
        lax.fori_loop(0, qi * (n_sub // MOBA_UNROLL), body, 0)
    acc0 = acc_ref[0]
    acc1 = acc_ref[1]
    out0 = acc0 / acc0[:, HEAD_DIM:HEAD_DIM + 1]
    out1 = acc1 / acc1[:, 0:1]
    o_ref[...] = jnp.where(lane_q < HEAD_DIM, out0, out1).astype(o_ref.dtype)


def _attn_kernel(slopes_ref, qs_ref, ks_ref, vs_ref, qm_ref, km_ref, vm_ref, os_ref, om_ref,
                 qh_ref, accs_ref, c_ref, kmean_ref, kaug_ref, vaug_ref, qaug_ref, m_ref,
                 accm_ref, *, n_blk, n_sub):
    moba = _moba_phases(slopes_ref, qm_ref, km_ref, vm_ref, om_ref, kmean_ref, kaug_ref,
                        vaug_ref, qaug_ref, m_ref, accm_ref, n_blk=n_blk, n_sub=n_sub)
    sb = _sb_phases(qs_ref, ks_ref, vs_ref, os_ref, qh_ref, accs_ref, c_ref, n_sub=n_sub)
    next(moba)
    next(sb)
    for _ in moba:
        pass
    for _ in sb:
        pass


def _attention(proj, slopes, n_sub):
    b, _, s, _ = proj.shape
    blk = MOBA_BLOCK
    n_blk = s // blk
    tq = n_sub * blk
    assert SB_KBLOCK == MOBA_BLOCK and n_blk <= MOBA_MAX_BLOCKS and s % tq == 0
    kernel = functools.partial(_attn_kernel, n_blk=n_blk, n_sub=n_sub)
    base = 4 * N_PAIRS
    whole = lambda group0: pl.BlockSpec((None, None, s, PAIR),
                                        lambda bi, hp, qi: (bi, group0 + hp, 0, 0))
    out = jax.ShapeDtypeStruct((b, N_PAIRS, s, PAIR), F32)
    return pl.pallas_call(
        kernel,
        grid=(b, N_PAIRS, s // tq),
        in_specs=[
            pl.BlockSpec(memory_space=pltpu.SMEM),
            _slab(tq, 0), whole(N_PAIRS), whole(2 * N_PAIRS),
            _slab(tq, base), whole(base + N_PAIRS), whole(base + 2 * N_PAIRS),
        ],
        out_specs=[_slab(tq, 0), _slab(tq, 0)],
        out_shape=[out, out],
        scratch_shapes=[
            pltpu.VMEM((2, tq, PAIR), BF16),
            pltpu.VMEM((2, tq, PAIR), F32),
            pltpu.VMEM((2, tq, 1), F32),
            pltpu.VMEM((MOBA_MAX_BLOCKS, PAIR), F32),
            pltpu.VMEM((2, s, PAIR), BF16),
            pltpu.VMEM((2, s, PAIR), BF16),
            pltpu.VMEM((2, tq, PAIR), BF16),
            pltpu.VMEM((2, tq, PAIR), F32),
            pltpu.VMEM((2, tq, PAIR), F32),
        ],
        compiler_params=pltpu.CompilerParams(
            dimension_semantics=("arbitrary", "arbitrary", "arbitrary"),
            vmem_limit_bytes=V7X_VMEM_LIMIT),
    )(slopes, proj, proj, proj, proj, proj, proj)


def _out_kernel(x_ref, osb_ref, omb_ref, zsb_ref, zmb_ref, p_ref,
                gsb_ref, gmb_ref, wout_ref, gple_ref, wgate_ref, wple_ref, gfin_ref, o_ref):
    def wide(r):
        return jnp.concatenate([r[c] for c in range(N_PAIRS)], axis=-1)

    def gated(o_r, z_r, g_r):
        z = wide(z_r).astype(F32)
        return _rms(wide(o_r), g_r[...]) * (z * jax.nn.sigmoid(z))

    y = jnp.concatenate([gated(osb_ref, zsb_ref, gsb_ref),
                         gated(omb_ref, zmb_ref, gmb_ref)], axis=-1)
    x1 = x_ref[...] + jnp.dot(y.astype(BF16), wout_ref[...], preferred_element_type=F32)
    gate = jax.nn.sigmoid(jnp.dot(_rms(x1, gple_ref[...]).astype(BF16), wgate_ref[...],
                                  preferred_element_type=F32))
    ple = jnp.dot(p_ref[...].astype(BF16), wple_ref[...], preferred_element_type=F32)
    x2 = x1 + ple * gate
    o_ref[...] = _rms(x2, gfin_ref[...])


def _out_proj(x2d, o_sb, o_mb, proj, p2d, g_sb, g_mb, w_out, g_ple, w_gate, w_ple, g_fin, tm):
    m, d = x2d.shape
    d_ple = p2d.shape[1]
    s = proj.shape[2]
    spb = s // tm
    row = lambda width: pl.BlockSpec((tm, width), lambda i: (i, 0))
    full = lambda a: pl.BlockSpec(a.shape, lambda i: (0, 0))
    groups = lambda gb: pl.BlockSpec((None, N_PAIRS, tm, PAIR),
                                     lambda i: (i // spb, gb, i % spb, 0))
    z_sb_block = 3
    z_mb_block = 7
    return pl.pallas_call(
        _out_kernel,
        grid=(m // tm,),
        in_specs=[
            row(d), groups(0), groups(0), groups(z_sb_block), groups(z_mb_block),
            row(d_ple), full(g_sb), full(g_mb), full(w_out), full(g_ple), full(w_gate),
            full(w_ple), full(g_fin),
        ],
        out_specs=row(d),
        out_shape=jax.ShapeDtypeStruct((m, d), F32),
        compiler_params=pltpu.CompilerParams(
            dimension_semantics=("arbitrary",), vmem_limit_bytes=V7X_VMEM_LIMIT),
    )(x2d, o_sb, o_mb, proj, proj, p2d, g_sb, g_mb, w_out, g_ple, w_gate, w_ple, g_fin)


def _alibi_slopes(n_heads):
    return jnp.asarray(np.power(2.0, -8.0 * np.arange(1, n_heads + 1) / n_heads).astype(np.float32))


def kernel(x, p, w_in, g_mix, g_out_sb, g_out_mb, w_out, w_ple, g_ple, w_ple_gate, g_final):
    b, s, d = x.shape
    assert w_in.shape[0] == 1, "single-layer trunk only"
    m = b * s
    slopes = _alibi_slopes(N_HEADS)
    x2d = x.reshape(m, d)
    proj = _proj(x2d, g_mix[0][None, :], w_in[0].astype(BF16), b, min(PROJ_ROWS, s))
    o_sb, o_mb = _attention(proj, slopes, min(ATTN_QSUB, s // MOBA_BLOCK))
    out = _out_proj(x2d, o_sb, o_mb, proj, p[0].reshape(m, -1),
                    g_out_sb[0][None, :], g_out_mb[0][None, :], w_out[0].astype(BF16),
                    g_ple[0][None, :], w_ple_gate[0].astype(BF16), w_ple[0].astype(BF16),
                    g_final[None, :], min(OUT_ROWS, s))
    return out.reshape(b, s, d)
```

```python
import functools
import math

import jax
import jax.numpy as jnp
import numpy as np
from jax import lax
from jax.experimental import pallas as pl
from jax.experimental.pallas import tpu as pltpu

HEAD_DIM = 64
N_HEADS = 8
D_GRP = N_HEADS * HEAD_DIM
PAIR = 2 * HEAD_DIM
N_PAIRS = D_GRP // PAIR
MOBA_BLOCK = 256
MOBA_TOPK = 3
ATTN_QSUB = 8
MOBA_UNROLL = 8
MOBA_MAX_BLOCKS = 16
SB_KBLOCK = 256
SB_DEAD = 105.0
PROJ_ROWS = 512
OUT_ROWS = 512
RMS_EPS = 1e-6
SCALE = 1.0 / math.sqrt(HEAD_DIM)
LOG2E = 1.4426950408889634
NEG = -1e30
V7X_VMEM_LIMIT = 56 * 1024 * 1024

F32 = jnp.float32
BF16 = jnp.bfloat16

_NT = (((1,), (1,)), ((), ()))


def _rms(x, g):
    ms = jnp.mean(x * x, axis=-1, keepdims=True)
    return (x * lax.rsqrt(ms + RMS_EPS)) * g


def _split3(a):
    hi = a.astype(BF16)
    r1 = a - hi.astype(F32)
    mid = r1.astype(BF16)
    lo = (r1 - mid.astype(F32)).astype(BF16)
    return hi, mid, lo


def _proj_kernel(x_ref, g_ref, w_ref, o_ref):
    h = _rms(x_ref[...], g_ref[...])
    res = jnp.dot(h.astype(BF16), w_ref[...], preferred_element_type=F32).astype(o_ref.dtype)
    for cb in range(o_ref.shape[0]):
        o_ref[cb] = res[:, cb * PAIR:(cb + 1) * PAIR]


def _proj(x2d, g, w_bf16, b, tm):
    m, d = x2d.shape
    n = w_bf16.shape[1]
    s = m // b
    spb = s // tm
    return pl.pallas_call(
        _proj_kernel,
        grid=(m // tm,),
        in_specs=[
            pl.BlockSpec((tm, d), lambda i: (i, 0)),
            pl.BlockSpec((1, d), lambda i: (0, 0)),
            pl.BlockSpec((d, n), lambda i: (0, 0)),
        ],
        out_specs=pl.BlockSpec((None, n // PAIR, tm, PAIR), lambda i: (i // spb, 0, i % spb, 0)),
        out_shape=jax.ShapeDtypeStruct((b, n // PAIR, s, PAIR), BF16),
        compiler_params=pltpu.CompilerParams(
            dimension_semantics=("arbitrary",), vmem_limit_bytes=V7X_VMEM_LIMIT),
    )(x2d, g, w_bf16)


def _sb_phases(q_ref, k_ref, v_ref, o_ref, qh_ref, acc_ref, c_ref, *, n_sub):
    blk = SB_KBLOCK
    tq = n_sub * blk
    qi = pl.program_id(2)
    lane = lax.broadcasted_iota(jnp.int32, (tq, PAIR), 1)
    q2 = q_ref[...] * SCALE
    zero = jnp.zeros_like(q2)
    qh_ref[0] = jnp.where(lane < HEAD_DIM, q2, zero)
    qh_ref[1] = jnp.where(lane >= HEAD_DIM, q2, zero)
    acc_ref[...] = jnp.zeros_like(acc_ref)
    c_ref[...] = jnp.zeros_like(c_ref)
    row = lax.broadcasted_iota(jnp.int32, (blk, blk), 0)
    col = lax.broadcasted_iota(jnp.int32, (blk, blk), 1)
    tri = jnp.where(row > col, 1.0, 0.0).astype(BF16)

    past = (lax.broadcasted_iota(jnp.int32, (blk, blk), 1)
            < lax.broadcasted_iota(jnp.int32, (blk, blk), 0))

    def mask_own(x, rows):
        parts = []
        for h in range(2):
            parts.append(jnp.where(past, x[h * rows:h * rows + blk], 0.0))
            if rows > blk:
                parts.append(x[h * rows + blk:(h + 1) * rows])
        return jnp.concatenate(parts, axis=0)

    def unit(j, row0, rows, diag):
        start = pl.multiple_of(j * blk, blk)
        k2 = k_ref[pl.ds(start, blk), :]
        v2 = v_ref[pl.ds(start, blk), :]
        rs = pl.ds(row0, rows)
        both = lambda ref: jnp.concatenate([ref[0, rs, :], ref[1, rs, :]], axis=0)
        z = lax.dot_general(both(qh_ref), k2, _NT, preferred_element_type=F32)
        sp = jnp.maximum(z, 0.0) + jnp.log(1.0 + jnp.exp2(jnp.abs(z) * (-LOG2E)))
        if diag:
            sp = mask_own(sp, rows)
        cs = jnp.dot(sp.astype(BF16), tri, preferred_element_type=F32)
        w = jnp.exp((z - sp) - cs)
        if diag:
            w = mask_own(w, rows)
        pv = jnp.dot(w.astype(BF16), v2, preferred_element_type=F32)
        total = cs[:, 0:1] + sp[:, 0:1]
        for h in range(2):
            c_in = c_ref[h, rs, :]
            acc_ref[h, rs, :] += jnp.exp(-c_in) * pv[h * rows:(h + 1) * rows]
            c_ref[h, rs, :] = c_in + total[h * rows:(h + 1) * rows]

    for r in reversed(range(n_sub)):
        unit(qi * n_sub + r, r * blk, min(2 * blk, tq - r * blk), True)

    yield

    @pl.when(qi > 0)
    def _():
        unit(qi * n_sub - 1, 0, blk, False)

    @pl.when(jnp.min(c_ref[...]) <= SB_DEAD)
    def _():
        def walk(r, carry):
            row0 = pl.multiple_of(r * blk, blk)

            def cond(j):
                alive = jnp.min(c_ref[:, pl.ds(row0, blk), :]) <= SB_DEAD
                return jnp.logical_and(j >= 0, alive)

            def body(j):
                unit(j, row0, blk, False)
                return j - 1

            lax.while_loop(cond, body, qi * n_sub + r - 2)
            return carry

        lax.fori_loop(0, n_sub, walk, 0)

    o_ref[...] = jnp.where(lane < HEAD_DIM, acc_ref[0], acc_ref[1]).astype(o_ref.dtype)


def _slab(rows, group0):
    return pl.BlockSpec((None, None, rows, PAIR), lambda bi, hp, qi: (bi, group0 + hp, qi, 0))


def _moba_phases(slopes_ref, q_ref, k_ref, v_ref, o_ref,
                 kmean_ref, kaug_ref, vaug_ref, qaug_ref, m_ref, acc_ref, *, n_blk, n_sub):
    blk = MOBA_BLOCK
    tq = n_sub * blk
    hp = pl.program_id(1)
    qi = pl.program_id(2)
    ext0 = (HEAD_DIM, 0)

    @pl.when(qi == 0)
    def _():
        lane = lax.broadcasted_iota(jnp.int32, (blk, PAIR), 1)
        pos = lax.broadcasted_iota(jnp.int32, (blk, PAIR), 0)
        kmean_ref[...] = jnp.zeros_like(kmean_ref)
        for n in range(n_blk):
            k2 = k_ref[n * blk:(n + 1) * blk, :]
            v2 = v_ref[n * blk:(n + 1) * blk, :]
            kmean_ref[n:n + 1, :] = jnp.mean(k2.astype(F32), axis=0, keepdims=True)
            for h in range(2):
                own = (lane < HEAD_DIM) if h == 0 else (lane >= HEAD_DIM)
                rel = lane - ext0[h]
                a_hi, a_mid, a_lo = _split3(slopes_ref[2 * hp + h] * (pos + n * blk).astype(F32))
                ext = jnp.where(rel == n, 1.0, 0.0).astype(BF16)
                ext = jnp.where(rel == MOBA_MAX_BLOCKS, a_hi, ext)
                ext = jnp.where(rel == MOBA_MAX_BLOCKS + 1, a_mid, ext)
                ext = jnp.where(rel == MOBA_MAX_BLOCKS + 2, a_lo, ext)
                kaug_ref[h, n * blk:(n + 1) * blk, :] = jnp.where(own, k2, ext)
                vaug_ref[h, n * blk:(n + 1) * blk, :] = jnp.where(
                    own, v2, jnp.where(rel == 0, 1.0, 0.0).astype(BF16))

    q_raw = q_ref[...]
    q2 = q_raw * SCALE
    lane_q = lax.broadcasted_iota(jnp.int32, (tq, PAIR), 1)
    km = kmean_ref[...]
    lane_k = lax.broadcasted_iota(jnp.int32, km.shape, 1)
    n_idx = lax.broadcasted_iota(jnp.int32, (MOBA_MAX_BLOCKS, tq), 0)
    own_blk = qi * n_sub + lax.broadcasted_iota(jnp.int32, (MOBA_MAX_BLOCKS, tq), 1) // blk
    row128 = lax.broadcasted_iota(jnp.int32, (PAIR, tq), 0)
    for h in range(2):
        kmh = jnp.where((lane_k < HEAD_DIM) == (h == 0), km, 0.0)
        hi = kmh.astype(BF16)
        lo = (kmh - hi.astype(F32)).astype(BF16)
        gt = (lax.dot_general(hi, q_raw, _NT, preferred_element_type=F32)
              + lax.dot_general(lo, q_raw, _NT, preferred_element_type=F32))
        gt = jnp.where(n_idx < own_blk, gt, -jnp.inf)
        bias_t = jnp.zeros((MOBA_MAX_BLOCKS, tq), F32)
        for j in range(n_blk):
            gj = gt[j:j + 1, :]
            beats = (jnp.where(gt > gj, 1.0, 0.0)
                     + jnp.where((gt == gj) & (n_idx < j), 1.0, 0.0))
            cnt = jnp.sum(beats, axis=0, keepdims=True)
            own_row = own_blk[0:1, :]
            drop = ((cnt >= MOBA_TOPK) & (own_row > j)) | (own_row < j)
            bias_t = jnp.where((n_idx == j) & drop, NEG, bias_t)
        top = ext0[h]
        pieces = [bias_t, jnp.zeros((PAIR - top - MOBA_MAX_BLOCKS, tq), F32)]
        if top:
            pieces.insert(0, jnp.zeros((top, tq), F32))
        ext_t = jnp.concatenate(pieces, axis=0)
        rel = row128 - top
        ext_t = jnp.where((rel >= MOBA_MAX_BLOCKS) & (rel < MOBA_MAX_BLOCKS + 3), 1.0, ext_t)
        own = (lane_q < HEAD_DIM) if h == 0 else (lane_q >= HEAD_DIM)
        qaug_ref[h] = jnp.where(own, q2, ext_t.T.astype(BF16))

    m_ref[...] = jnp.full_like(m_ref, NEG)
    acc_ref[...] = jnp.zeros_like(acc_ref)
    causal = (lax.broadcasted_iota(jnp.int32, (blk, blk), 1)
              <= lax.broadcasted_iota(jnp.int32, (blk, blk), 0))

    def block(j, row0, diag):
        start = pl.multiple_of(j * blk, blk)
        rows = tq - row0
        for h in range(2):
            ka = kaug_ref[h, pl.ds(start, blk), :]
            va = vaug_ref[h, pl.ds(start, blk), :]
            s = lax.dot_general(qaug_ref[h, row0:, :], ka, _NT, preferred_element_type=F32)
            if diag:
                own = jnp.where(causal, s[:blk], NEG)
                s = own if rows == blk else jnp.concatenate([own, s[blk:]], axis=0)
            m_old = m_ref[h, row0:, :]
            m_new = jnp.maximum(m_old, jnp.max(s, axis=1, keepdims=True))
            alpha = jnp.exp(m_old - m_new)
            p = jnp.exp(s - jnp.concatenate([m_new] * (blk // PAIR), axis=1))
            acc_ref[h, row0:, :] = (alpha * acc_ref[h, row0:, :]
                                    + jnp.dot(p.astype(BF16), va, preferred_element_type=F32))
            m_ref[h, row0:, :] = m_new

    for d in reversed(range(n_sub)):
        block(qi * n_sub + d, d * blk, True)

    yield

    def body(i, carry):
        for u in range(MOBA_UNROLL):
            block(i * MOBA_UNROLL + u, 0, False)
        return carry

    if n_sub < n_blk:
        lax.fori_loop(0, qi * (n_sub // MOBA_UNROLL), body, 0)
    acc0 = acc_ref[0]
    acc1 = acc_ref[1]
    out0 = acc0 / acc0[:, HEAD_DIM:HEAD_DIM + 1]
    out1 = acc1 / acc1[:, 0:1]
    o_ref[...] = jnp.where(lane_q < HEAD_DIM, out0, out1).astype(o_ref.dtype)


def _attn_kernel(slopes_ref, qs_ref, ks_ref, vs_ref, qm_ref, km_ref, vm_ref, os_ref, om_ref,
                 qh_ref, accs_ref, c_ref, kmean_ref, kaug_ref, vaug_ref, qaug_ref, m_ref,
                 accm_ref, *, n_blk, n_sub):
    moba = _moba_phases(slopes_ref, qm_ref, km_ref, vm_ref, om_ref, kmean_ref, kaug_ref,
                        vaug_ref, qaug_ref, m_ref, accm_ref, n_blk=n_blk, n_sub=n_sub)
    sb = _sb_phases(qs_ref, ks_ref, vs_ref, os_ref, qh_ref, accs_ref, c_ref, n_sub=n_sub)
    next(moba)
    next(sb)
    for _ in moba:
        pass
    for _ in sb:
        pass


def _attention(proj, slopes, n_sub):
    b, _, s, _ = proj.shape
    blk = MOBA_BLOCK
    n_blk = s // blk
    tq = n_sub * blk
    assert SB_KBLOCK == MOBA_BLOCK and n_blk <= MOBA_MAX_BLOCKS and s % tq == 0
    kernel = functools.partial(_attn_kernel, n_blk=n_blk, n_sub=n_sub)
    base = 4 * N_PAIRS
    whole = lambda group0: pl.BlockSpec((None, None, s, PAIR),
                                        lambda bi, hp, qi: (bi, group0 + hp, 0, 0))
    out = jax.ShapeDtypeStruct((b, N_PAIRS, s, PAIR), F32)
    return pl.pallas_call(
        kernel,
        grid=(b, N_PAIRS, s // tq),
        in_specs=[
            pl.BlockSpec(memory_space=pltpu.SMEM),
            _slab(tq, 0), whole(N_PAIRS), whole(2 * N_PAIRS),
            _slab(tq, base), whole(base + N_PAIRS), whole(base + 2 * N_PAIRS),
        ],
        out_specs=[_slab(tq, 0), _slab(tq, 0)],
        out_shape=[out, out],
        scratch_shapes=[
            pltpu.VMEM((2, tq, PAIR), BF16),
            pltpu.VMEM((2, tq, PAIR), F32),
            pltpu.VMEM((2, tq, 1), F32),
            pltpu.VMEM((MOBA_MAX_BLOCKS, PAIR), F32),
            pltpu.VMEM((2, s, PAIR), BF16),
            pltpu.VMEM((2, s, PAIR), BF16),
            pltpu.VMEM((2, tq, PAIR), BF16),
            pltpu.VMEM((2, tq, PAIR), F32),
            pltpu.VMEM((2, tq, PAIR), F32),
        ],
        compiler_params=pltpu.CompilerParams(
            dimension_semantics=("arbitrary", "arbitrary", "arbitrary"),
            vmem_limit_bytes=V7X_VMEM_LIMIT),
    )(slopes, proj, proj, proj, proj, proj, proj)


def _out_kernel(x_ref, osb_ref, omb_ref, zsb_ref, zmb_ref, p_ref,
                gsb_ref, gmb_ref, wout_ref, gple_ref, wgate_ref, wple_ref, gfin_ref, o_ref):
    def wide(r):
        return jnp.concatenate([r[c] for c in range(N_PAIRS)], axis=-1)

    def gated(o_r, z_r, g_r):
        z = wide(z_r).astype(F32)
        return _rms(wide(o_r), g_r[...]) * (z * jax.nn.sigmoid(z))

    y = jnp.concatenate([gated(osb_ref, zsb_ref, gsb_ref),
                         gated(omb_ref, zmb_ref, gmb_ref)], axis=-1)
    x1 = x_ref[...] + jnp.dot(y.astype(BF16), wout_ref[...], preferred_element_type=F32)
    gate = jax.nn.sigmoid(jnp.dot(_rms(x1, gple_ref[...]).astype(BF16), wgate_ref[...],
                                  preferred_element_type=F32))
    ple = jnp.dot(p_ref[...].astype(BF16), wple_ref[...], preferred_element_type=F32)
    x2 = x1 + ple * gate
    o_ref[...] = _rms(x2, gfin_ref[...])


def _out_proj(x2d, o_sb, o_mb, proj, p2d, g_sb, g_mb, w_out, g_ple, w_gate, w_ple, g_fin, tm):
    m, d = x2d.shape
    d_ple = p2d.shape[1]
    s = proj.shape[2]
    spb = s // tm
    row = lambda width: pl.BlockSpec((tm, width), lambda i: (i, 0))
    full = lambda a: pl.BlockSpec(a.shape, lambda i: (0, 0))
    groups = lambda gb: pl.BlockSpec((None, N_PAIRS, tm, PAIR),
                                     lambda i: (i // spb, gb, i % spb, 0))
    z_sb_block = 3
    z_mb_block = 7
    return pl.pallas_call(
        _out_kernel,
        grid=(m // tm,),
        in_specs=[
            row(d), groups(0), groups(0), groups(z_sb_block), groups(z_mb_block),
            row(d_ple), full(g_sb), full(g_mb), full(w_out), full(g_ple), full(w_gate),
            full(w_ple), full(g_fin),
        ],
        out_specs=row(d),
        out_shape=jax.ShapeDtypeStruct((m, d), F32),
        compiler_params=pltpu.CompilerParams(
            dimension_semantics=("arbitrary",), vmem_limit_bytes=V7X_VMEM_LIMIT),
    )(x2d, o_sb, o_mb, proj, proj, p2d, g_sb, g_mb, w_out, g_ple, w_gate, w_ple, g_fin)


def _alibi_slopes(n_heads):
    return jnp.asarray(np.power(2.0, -8.0 * np.arange(1, n_heads + 1) / n_heads).astype(np.float32))


def kernel(x, p, w_in, g_mix, g_out_sb, g_out_mb, w_out, w_ple, g_ple, w_ple_gate, g_final):
    b, s, d = x.shape
    assert w_in.shape[0] == 1, "single-layer trunk only"
    m = b * s
    slopes = _alibi_slopes(N_HEADS)
    x2d = x.reshape(m, d)
    proj = _proj(x2d, g_mix[0][None, :], w_in[0].astype(BF16), b, min(PROJ_ROWS, s))
    o_sb, o_mb = _attention(proj, slopes, min(ATTN_QSUB, s // MOBA_BLOCK))
    out = _out_proj(x2d, o_sb, o_mb, proj, p[0].reshape(m, -1),
                    g_out_sb[0][None, :], g_out_mb[0][None, :], w_out[0].astype(BF16),
                    g_ple[0][None, :], w_ple_gate[0].astype(BF16), w_ple[0].astype(BF16),
                    g_final[None, :], min(OUT_ROWS, s))
    return out.reshape(b, s, d)
```

```python
import functools
import math

import jax
import jax.numpy as jnp
import numpy as np
from jax import lax
from jax.experimental import pallas as pl
from jax.experimental.pallas import tpu as pltpu

HEAD_DIM = 64
N_HEADS = 8
D_GRP = N_HEADS * HEAD_DIM
PAIR = 2 * HEAD_DIM
N_PAIRS = D_GRP // PAIR
MOBA_BLOCK = 256
MOBA_TOPK = 3
ATTN_QSUB = 8
MOBA_UNROLL = 8
MOBA_MAX_BLOCKS = 16
SB_KBLOCK = 256
SB_DEAD = 105.0
PROJ_ROWS = 512
OUT_ROWS = 512
RMS_EPS = 1e-6
SCALE = 1.0 / math.sqrt(HEAD_DIM)
LOG2E = 1.4426950408889634
NEG = -1e30
V7X_VMEM_LIMIT = 56 * 1024 * 1024

F32 = jnp.float32
BF16 = jnp.bfloat16

_NT = (((1,), (1,)), ((), ()))


def _rms(x, g):
    ms = jnp.mean(x * x, axis=-1, keepdims=True)
    return (x * lax.rsqrt(ms + RMS_EPS)) * g


def _split3(a):
    hi = a.astype(BF16)
    r1 = a - hi.astype(F32)
    mid = r1.astype(BF16)
    lo = (r1 - mid.astype(F32)).astype(BF16)
    return hi, mid, lo


def _proj_kernel(x_ref, g_ref, w_ref, o_ref):
    h = _rms(x_ref[...], g_ref[...])
    res = jnp.dot(h.astype(BF16), w_ref[...], preferred_element_type=F32).astype(o_ref.dtype)
    for cb in range(o_ref.shape[0]):
        o_ref[cb] = res[:, cb * PAIR:(cb + 1) * PAIR]


def _proj(x2d, g, w_bf16, b, tm):
    m, d = x2d.shape
    n = w_bf16.shape[1]
    s = m // b
    spb = s // tm
    return pl.pallas_call(
        _proj_kernel,
        grid=(m // tm,),
        in_specs=[
            pl.BlockSpec((tm, d), lambda i: (i, 0)),
            pl.BlockSpec((1, d), lambda i: (0, 0)),
            pl.BlockSpec((d, n), lambda i: (0, 0)),
        ],
        out_specs=pl.BlockSpec((None, n // PAIR, tm, PAIR), lambda i: (i // spb, 0, i % spb, 0)),
        out_shape=jax.ShapeDtypeStruct((b, n // PAIR, s, PAIR), BF16),
        compiler_params=pltpu.CompilerParams(
            dimension_semantics=("arbitrary",), vmem_limit_bytes=V7X_VMEM_LIMIT),
    )(x2d, g, w_bf16)


def _sb_phases(q_ref, k_ref, v_ref, o_ref, qh_ref, acc_ref, c_ref, *, n_sub):
    blk = SB_KBLOCK
    tq = n_sub * blk
    qi = pl.program_id(2)
    lane = lax.broadcasted_iota(jnp.int32, (tq, PAIR), 1)
    q2 = q_ref[...] * SCALE
    zero = jnp.zeros_like(q2)
    qh_ref[0] = jnp.where(lane < HEAD_DIM, q2, zero)
    qh_ref[1] = jnp.where(lane >= HEAD_DIM, q2, zero)
    acc_ref[...] = jnp.zeros_like(acc_ref)
    c_ref[...] = jnp.zeros_like(c_ref)
    row = lax.broadcasted_iota(jnp.int32, (blk, blk), 0)
    col = lax.broadcasted_iota(jnp.int32, (blk, blk), 1)
    tri = jnp.where(row > col, 1.0, 0.0).astype(BF16)

    past = (lax.broadcasted_iota(jnp.int32, (blk, blk), 1)
            < lax.broadcasted_iota(jnp.int32, (blk, blk), 0))

    def mask_own(x, rows):
        parts = []
        for h in range(2):
            parts.append(jnp.where(past, x[h * rows:h * rows + blk], 0.0))
            if rows > blk:
                parts.append(x[h * rows + blk:(h + 1) * rows])
        return jnp.concatenate(parts, axis=0)

    def unit(j, row0, rows, diag):
        start = pl.multiple_of(j * blk, blk)
        k2 = k_ref[pl.ds(start, blk), :]
        v2 = v_ref[pl.ds(start, blk), :]
        rs = pl.ds(row0, rows)
        both = lambda ref: jnp.concatenate([ref[0, rs, :], ref[1, rs, :]], axis=0)
        z = lax.dot_general(both(qh_ref), k2, _NT, preferred_element_type=F32)
        sp = jnp.maximum(z, 0.0) + jnp.log(1.0 + jnp.exp2(jnp.abs(z) * (-LOG2E)))
        if diag:
            sp = mask_own(sp, rows)
        cs = jnp.dot(sp.astype(BF16), tri, preferred_element_type=F32)
        w = jnp.exp((z - sp) - cs)
        if diag:
            w = mask_own(w, rows)
        pv = jnp.dot(w.astype(BF16), v2, preferred_element_type=F32)
        total = cs[:, 0:1] + sp[:, 0:1]
        for h in range(2):
            c_in = c_ref[h, rs, :]
            acc_ref[h, rs, :] += jnp.exp(-c_in) * pv[h * rows:(h + 1) * rows]
            c_ref[h, rs, :] = c_in + total[h * rows:(h + 1) * rows]

    for r in reversed(range(n_sub)):
        unit(qi * n_sub + r, r * blk, min(2 * blk, tq - r * blk), True)

    yield

    @pl.when(qi > 0)
    def _():
        unit(qi * n_sub - 1, 0, blk, False)

    @pl.when(jnp.min(c_ref[...]) <= SB_DEAD)
    def _():
        def walk(r, carry):
            row0 = pl.multiple_of(r * blk, blk)

            def cond(j):
                alive = jnp.min(c_ref[:, pl.ds(row0, blk), :]) <= SB_DEAD
                return jnp.logical_and(j >= 0, alive)

            def body(j):
                unit(j, row0, blk, False)
                return j - 1

            lax.while_loop(cond, body, qi * n_sub + r - 2)
            return carry

        lax.fori_loop(0, n_sub, walk, 0)

    o_ref[...] = jnp.where(lane < HEAD_DIM, acc_ref[0], acc_ref[1]).astype(o_ref.dtype)


def _slab(rows, group0):
    return pl.BlockSpec((None, None, rows, PAIR), lambda bi, hp, qi: (bi, group0 + hp, qi, 0))


def _moba_phases(slopes_ref, q_ref, k_ref, v_ref, o_ref,
                 kmean_ref, kaug_ref, vaug_ref, qaug_ref, m_ref, acc_ref, *, n_blk, n_sub):
    blk = MOBA_BLOCK
    tq = n_sub * blk
    hp = pl.program_id(1)
    qi = pl.program_id(2)
    ext0 = (HEAD_DIM, 0)

    @pl.when(qi == 0)
    def _():
        lane = lax.broadcasted_iota(jnp.int32, (blk, PAIR), 1)
        pos = lax.broadcasted_iota(jnp.int32, (blk, PAIR), 0)
        kmean_ref[...] = jnp.zeros_like(kmean_ref)
        owns, rels, in_block, ones_lane = [], [], [], []
        for h in range(2):
            owns.append((lane < HEAD_DIM) if h == 0 else (lane >= HEAD_DIM))
            rels.append(lane - ext0[h])
            p_hi, p_mid, p_lo = (t.astype(F32) for t in _split3(slopes_ref[2 * hp + h] * pos.astype(F32)))
            ext = jnp.where(rels[h] == MOBA_MAX_BLOCKS, p_hi, 0.0)
            ext = jnp.where(rels[h] == MOBA_MAX_BLOCKS + 1, p_mid, ext)
            in_block.append(jnp.where(rels[h] == MOBA_MAX_BLOCKS + 2, p_lo, ext))
            ones_lane.append(jnp.where(rels[h] == 0, 1.0, 0.0).astype(BF16))
        for n in range(n_blk):
            k2 = k_ref[n * blk:(n + 1) * blk, :]
            v2 = v_ref[n * blk:(n + 1) * blk, :]
            kmean_ref[n:n + 1, :] = jnp.mean(k2.astype(F32), axis=0, keepdims=True)
            for h in range(2):
                rel = rels[h]
                start = jnp.full((1, PAIR), slopes_ref[2 * hp + h] * float(n * blk), F32)
                s_hi, s_mid, s_lo = (piece.astype(F32) for piece in _split3(start))
                ext = jnp.where(rel == n, 1.0, in_block[h])
                ext = jnp.where(rel == MOBA_MAX_BLOCKS + 3, s_hi, ext)
                ext = jnp.where(rel == MOBA_MAX_BLOCKS + 4, s_mid, ext)
                ext = jnp.where(rel == MOBA_MAX_BLOCKS + 5, s_lo, ext).astype(BF16)
                kaug_ref[h, n * blk:(n + 1) * blk, :] = jnp.where(owns[h], k2, ext)
                vaug_ref[h, n * blk:(n + 1) * blk, :] = jnp.where(owns[h], v2, ones_lane[h])

    q_raw = q_ref[...]
    q2 = q_raw * SCALE
    lane_q = lax.broadcasted_iota(jnp.int32, (tq, PAIR), 1)
    km = kmean_ref[...]
    lane_k = lax.broadcasted_iota(jnp.int32, km.shape, 1)
    n_idx = lax.broadcasted_iota(jnp.int32, (MOBA_MAX_BLOCKS, tq), 0)
    own_blk = qi * n_sub + lax.broadcasted_iota(jnp.int32, (MOBA_MAX_BLOCKS, tq), 1) // blk
    row128 = lax.broadcasted_iota(jnp.int32, (PAIR, tq), 0)
    for h in range(2):
        kmh = jnp.where((lane_k < HEAD_DIM) == (h == 0), km, 0.0)
        hi = kmh.astype(BF16)
        lo = (kmh - hi.astype(F32)).astype(BF16)
        gt = (lax.dot_general(hi, q_raw, _NT, preferred_element_type=F32)
              + lax.dot_general(lo, q_raw, _NT, preferred_element_type=F32))
        gt = jnp.where(n_idx < own_blk, gt, -jnp.inf)
        bias_t = jnp.zeros((MOBA_MAX_BLOCKS, tq), F32)
        for j in range(n_blk):
            gj = gt[j:j + 1, :]
            beats = (jnp.where(gt > gj, 1.0, 0.0)
                     + jnp.where((gt == gj) & (n_idx < j), 1.0, 0.0))
            cnt = jnp.sum(beats, axis=0, keepdims=True)
            own_row = own_blk[0:1, :]
            drop = ((cnt >= MOBA_TOPK) & (own_row > j)) | (own_row < j)
            bias_t = jnp.where((n_idx == j) & drop, NEG, bias_t)
        top = ext0[h]
        pieces = [bias_t, jnp.zeros((PAIR - top - MOBA_MAX_BLOCKS, tq), F32)]
        if top:
            pieces.insert(0, jnp.zeros((top, tq), F32))
        ext_t = jnp.concatenate(pieces, axis=0)
        rel = row128 - top
        ext_t = jnp.where((rel >= MOBA_MAX_BLOCKS) & (rel < MOBA_MAX_BLOCKS + 6), 1.0, ext_t)
        own = (lane_q < HEAD_DIM) if h == 0 else (lane_q >= HEAD_DIM)
        qaug_ref[h] = jnp.where(own, q2, ext_t.T.astype(BF16))

    m_ref[...] = jnp.full_like(m_ref, NEG)
    acc_ref[...] = jnp.zeros_like(acc_ref)
    causal = (lax.broadcasted_iota(jnp.int32, (blk, blk), 1)
              <= lax.broadcasted_iota(jnp.int32, (blk, blk), 0))

    def block(j, row0, diag):
        start = pl.multiple_of(j * blk, blk)
        rows = tq - row0
        for h in range(2):
            ka = kaug_ref[h, pl.ds(start, blk), :]
            va = vaug_ref[h, pl.ds(start, blk), :]
            s = lax.dot_general(qaug_ref[h, row0:, :], ka, _NT, preferred_element_type=F32)
            if diag:
                own = jnp.where(causal, s[:blk], NEG)
                s = own if rows == blk else jnp.concatenate([own, s[blk:]], axis=0)
            m_old = m_ref[h, row0:, :]
            m_new = jnp.maximum(m_old, jnp.max(s, axis=1, keepdims=True))
            alpha = jnp.exp(m_old - m_new)
            p = jnp.exp(s - jnp.concatenate([m_new] * (blk // PAIR), axis=1))
            acc_ref[h, row0:, :] = (alpha * acc_ref[h, row0:, :]
                                    + jnp.dot(p.astype(BF16), va, preferred_element_type=F32))
            m_ref[h, row0:, :] = m_new

    for d in reversed(range(n_sub)):
        block(qi * n_sub + d, d * blk, True)

    yield

    unroll = math.gcd(MOBA_UNROLL, n_sub)

    def body(i, carry):
        for u in range(unroll):
            block(i * unroll + u, 0, False)
        return carry

    if n_sub < n_blk:
        lax.fori_loop(0, qi * (n_sub // unroll), body, 0)
    acc0 = acc_ref[0]
    acc1 = acc_ref[1]
    out0 = acc0 / acc0[:, HEAD_DIM:HEAD_DIM + 1]
    out1 = acc1 / acc1[:, 0:1]
    o_ref[...] = jnp.where(lane_q < HEAD_DIM, out0, out1).astype(o_ref.dtype)


def _attn_kernel(slopes_ref, qs_ref, ks_ref, vs_ref, qm_ref, km_ref, vm_ref, os_ref, om_ref,
                 qh_ref, accs_ref, c_ref, kmean_ref, kaug_ref, vaug_ref, qaug_ref, m_ref,
                 accm_ref, *, n_blk, n_sub):
    moba = _moba_phases(slopes_ref, qm_ref, km_ref, vm_ref, om_ref, kmean_ref, kaug_ref,
                        vaug_ref, qaug_ref, m_ref, accm_ref, n_blk=n_blk, n_sub=n_sub)
    sb = _sb_phases(qs_ref, ks_ref, vs_ref, os_ref, qh_ref, accs_ref, c_ref, n_sub=n_sub)
    next(moba)
    next(sb)
    for _ in moba:
        pass
    for _ in sb:
        pass


def _attention(proj, slopes, n_sub):
    b, _, s, _ = proj.shape
    blk = MOBA_BLOCK
    n_blk = s // blk
    tq = n_sub * blk
    assert SB_KBLOCK == MOBA_BLOCK and n_blk <= MOBA_MAX_BLOCKS and s % tq == 0
    kernel = functools.partial(_attn_kernel, n_blk=n_blk, n_sub=n_sub)
    base = 4 * N_PAIRS
    whole = lambda group0: pl.BlockSpec((None, None, s, PAIR),
                                        lambda bi, hp, qi: (bi, group0 + hp, 0, 0))
    out = jax.ShapeDtypeStruct((b, N_PAIRS, s, PAIR), F32)
    return pl.pallas_call(
        kernel,
        grid=(b, N_PAIRS, s // tq),
        in_specs=[
            pl.BlockSpec(memory_space=pltpu.SMEM),
            _slab(tq, 0), whole(N_PAIRS), whole(2 * N_PAIRS),
            _slab(tq, base), whole(base + N_PAIRS), whole(base + 2 * N_PAIRS),
        ],
        out_specs=[_slab(tq, 0), _slab(tq, 0)],
        out_shape=[out, out],
        scratch_shapes=[
            pltpu.VMEM((2, tq, PAIR), BF16),
            pltpu.VMEM((2, tq, PAIR), F32),
            pltpu.VMEM((2, tq, 1), F32),
            pltpu.VMEM((MOBA_MAX_BLOCKS, PAIR), F32),
            pltpu.VMEM((2, s, PAIR), BF16),
            pltpu.VMEM((2, s, PAIR), BF16),
            pltpu.VMEM((2, tq, PAIR), BF16),
            pltpu.VMEM((2, tq, PAIR), F32),
            pltpu.VMEM((2, tq, PAIR), F32),
        ],
        compiler_params=pltpu.CompilerParams(
            dimension_semantics=("arbitrary", "arbitrary", "arbitrary"),
            vmem_limit_bytes=V7X_VMEM_LIMIT),
    )(slopes, proj, proj, proj, proj, proj, proj)


def _out_kernel(x_ref, osb_ref, omb_ref, zsb_ref, zmb_ref, p_ref,
                gsb_ref, gmb_ref, wout_ref, gple_ref, wgate_ref, wple_ref, gfin_ref, o_ref):
    def wide(r):
        return jnp.concatenate([r[c] for c in range(N_PAIRS)], axis=-1)

    def gated(o_r, z_r, g_r):
        z = wide(z_r).astype(F32)
        return _rms(wide(o_r), g_r[...]) * (z * jax.nn.sigmoid(z))

    y = jnp.concatenate([gated(osb_ref, zsb_ref, gsb_ref),
                         gated(omb_ref, zmb_ref, gmb_ref)], axis=-1)
    x1 = x_ref[...] + jnp.dot(y.astype(BF16), wout_ref[...], preferred_element_type=F32)
    gate = jax.nn.sigmoid(jnp.dot(_rms(x1, gple_ref[...]).astype(BF16), wgate_ref[...],
                                  preferred_element_type=F32))
    ple = jnp.dot(p_ref[...].astype(BF16), wple_ref[...], preferred_element_type=F32)
    x2 = x1 + ple * gate
    o_ref[...] = _rms(x2, gfin_ref[...])


def _out_proj(x2d, o_sb, o_mb, proj, p2d, g_sb, g_mb, w_out, g_ple, w_gate, w_ple, g_fin, tm):
    m, d = x2d.shape
    d_ple = p2d.shape[1]
    s = proj.shape[2]
    spb = s // tm
    row = lambda width: pl.BlockSpec((tm, width), lambda i: (i, 0))
    full = lambda a: pl.BlockSpec(a.shape, lambda i: (0, 0))
    groups = lambda gb: pl.BlockSpec((None, N_PAIRS, tm, PAIR),
                                     lambda i: (i // spb, gb, i % spb, 0))
    z_sb_block = 3
    z_mb_block = 7
    return pl.pallas_call(
        _out_kernel,
        grid=(m // tm,),
        in_specs=[
            row(d), groups(0), groups(0), groups(z_sb_block), groups(z_mb_block),
            row(d_ple), full(g_sb), full(g_mb), full(w_out), full(g_ple), full(w_gate),
            full(w_ple), full(g_fin),
        ],
        out_specs=row(d),
        out_shape=jax.ShapeDtypeStruct((m, d), F32),
        compiler_params=pltpu.CompilerParams(
            dimension_semantics=("arbitrary",), vmem_limit_bytes=V7X_VMEM_LIMIT),
    )(x2d, o_sb, o_mb, proj, proj, p2d, g_sb, g_mb, w_out, g_ple, w_gate, w_ple, g_fin)


def _alibi_slopes(n_heads):
    return jnp.asarray(np.power(2.0, -8.0 * np.arange(1, n_heads + 1) / n_heads).astype(np.float32))


def kernel(x, p, w_in, g_mix, g_out_sb, g_out_mb, w_out, w_ple, g_ple, w_ple_gate, g_final):
    b, s, d = x.shape
    assert w_in.shape[0] == 1, "single-layer trunk only"
    m = b * s
    slopes = _alibi_slopes(N_HEADS)
    x2d = x.reshape(m, d)
    proj = _proj(x2d, g_mix[0][None, :], w_in[0].astype(BF16), b, min(PROJ_ROWS, s))
    o_sb, o_mb = _attention(proj, slopes, min(ATTN_QSUB, s // MOBA_BLOCK))
    out = _out_proj(x2d, o_sb, o_mb, proj, p[0].reshape(m, -1),
                    g_out_sb[0][None, :], g_out_mb[0][None, :], w_out[0].astype(BF16),
                    g_ple[0][None, :], w_ple_gate[0].astype(BF16), w_ple[0].astype(BF16),
                    g_final[None, :], min(OUT_ROWS, s))
    return out.reshape(b, s, d)
```
